```python
import jax, jax.numpy as jnp
from jax import lax
import numpy as np

D_MODEL = 1024
BATCH = 4
SEQ = 4096
DEPTH = 1

MIX_WIDTH = D_MODEL
SB_WIDTH = MIX_WIDTH // 2
SB_HEAD_DIM = 64
SB_HEADS = SB_WIDTH // SB_HEAD_DIM
HG_WIDTH = MIX_WIDTH - SB_WIDTH
HG_HEAD_DIM = 128
HG_HEADS = HG_WIDTH // HG_HEAD_DIM
D_FF = 2816
Q_BLOCK = 128
HG_CHUNK = 64
EPS = 1e-6
IN_WIDTHS = (SB_WIDTH, SB_WIDTH, SB_WIDTH, HG_WIDTH, HG_WIDTH, HG_WIDTH, HG_WIDTH)
IN_COLS = sum(IN_WIDTHS)
IN_SPLITS = tuple(int(s) for s in np.cumsum(IN_WIDTHS)[:-1])

kernel_name = "hybrid_stickbreak_hgrn2_macaron"


def rmsnorm(x, gain):
    xf = x.astype(jnp.float32)
    y = xf * lax.rsqrt(jnp.mean(xf * xf, axis=-1, keepdims=True) + EPS)
    return (y * gain.astype(jnp.float32)).astype(x.dtype)


def head_rmsnorm(o, gain, n_heads, head_dim):
    B, T, _ = o.shape
    of = o.astype(jnp.float32).reshape(B, T, n_heads, head_dim)
    of = of * lax.rsqrt(jnp.mean(of * of, axis=-1, keepdims=True) + EPS)
    return of.reshape(B, T, n_heads * head_dim) * gain.astype(jnp.float32)


def swiglu(h, w_gate, w_up, w_down):
    return (jax.nn.silu(h @ w_gate) * (h @ w_up)) @ w_down


def split_heads(a, n_heads, head_dim):
    B, T, _ = a.shape
    return a.reshape(B, T, n_heads, head_dim).transpose(0, 2, 1, 3)


def merge_heads(a):
    B, H, T, Dh = a.shape
    return a.transpose(0, 2, 1, 3).reshape(B, T, H * Dh)


def stick_breaking_attention(q, k, v):
    B, H, T, Dh = q.shape
    n_blocks = T // Q_BLOCK
    scale = Dh ** -0.5
    q_blocks = q.reshape(B, H, n_blocks, Q_BLOCK, Dh).transpose(2, 0, 1, 3, 4)
    key_pos = jnp.arange(T)

    def block(args):
        q_blk, blk_idx = args
        z = jnp.einsum('bhqd,bhkd->bhqk', q_blk, k) * scale
        q_pos = blk_idx * Q_BLOCK + jnp.arange(Q_BLOCK)
        causal = key_pos[None, :] < q_pos[:, None]
        log_beta = jax.nn.log_sigmoid(z)
        log_rest = jnp.where(causal, jax.nn.log_sigmoid(-z), 0.0)
        tail = lax.cumsum(log_rest, axis=3, reverse=True) - log_rest
        w = jnp.where(causal, jnp.exp(log_beta + tail), 0.0)
        return jnp.einsum('bhqk,bhkd->bhqd', w, v)

    out = lax.map(block, (q_blocks, jnp.arange(n_blocks)))
    return out.transpose(1, 2, 0, 3, 4).reshape(B, H, T, Dh)


def hgrn2_chunkwise(q, k, v, log_f):
    B, H, T, Dk = q.shape
    Dv = v.shape[-1]
    n = T // HG_CHUNK

    def to_chunks(a):
        return a.reshape(B, H, n, HG_CHUNK, a.shape[-1]).transpose(2, 0, 1, 3, 4)

    qc, kc, vc = to_chunks(q), to_chunks(k), to_chunks(v)
    bc = lax.cumsum(to_chunks(log_f), axis=3)
    idx = jnp.arange(HG_CHUNK)
    incl = (idx[:, None] >= idx[None, :])[:, :, None]

    def step(S, xs):
        q_c, k_c, v_c, b_c = xs
        diff = b_c[:, :, :, None, :] - b_c[:, :, None, :, :]
        decay = jnp.where(incl, jnp.exp(jnp.minimum(diff, 0.0)), 0.0)
        scores = jnp.einsum('bhtd,bhsd,bhtsd->bhts', q_c, k_c, decay)
        o_intra = jnp.einsum('bhts,bhsv->bhtv', scores, v_c)
        o_inter = jnp.einsum('bhtd,bhdv->bhtv', q_c * jnp.exp(b_c), S)
        b_last = b_c[:, :, -1:, :]
        S_new = jnp.exp(b_last[:, :, 0, :])[..., None] * S + jnp.einsum(
            'bhsd,bhsv->bhdv', k_c * jnp.exp(b_last - b_c), v_c)
        return S_new, o_intra + o_inter

    S0 = jnp.zeros((B, H, Dk, Dv), jnp.float32)
    _, o = lax.scan(step, S0, (qc, kc, vc, bc))
    return o.transpose(1, 2, 0, 3, 4).reshape(B, H, T, Dv)


def setup_inputs(seed: int = 0) -> dict:
    key = jax.random.key(seed)
    ks = jax.random.split(key, 20)
    f32 = jnp.float32

    def normal(k, shape, scale):
        return jax.random.normal(k, shape, f32) * scale

    def gain(k, shape):
        return 1.0 + 0.02 * jax.random.normal(k, shape, f32)

    return {
        "x": jax.random.normal(ks[0], (BATCH, SEQ, D_MODEL), f32),
        "ffn1_norm": gain(ks[1], (DEPTH, D_MODEL)),
        "ffn1_w_gate": normal(ks[2], (DEPTH, D_MODEL, D_FF), D_MODEL ** -0.5),
        "ffn1_w_up": normal(ks[3], (DEPTH, D_MODEL, D_FF), D_MODEL ** -0.5),
        "ffn1_w_down": normal(ks[4], (DEPTH, D_FF, D_MODEL), D_FF ** -0.5),
        "mix_norm": gain(ks[5], (DEPTH, D_MODEL)),
        "w_in": normal(ks[6], (DEPTH, D_MODEL, IN_COLS), D_MODEL ** -0.5),
        "sb_out_norm": gain(ks[7], (DEPTH, SB_WIDTH)),
        "hg_lower_bound_logits": normal(ks[8], (DEPTH + 1, HG_WIDTH), 0.1),
        "hg_out_norm": gain(ks[9], (DEPTH, HG_WIDTH)),
        "w_out": normal(ks[10], (DEPTH, MIX_WIDTH, D_MODEL), MIX_WIDTH ** -0.5),
        "ffn2_norm": gain(ks[11], (DEPTH, D_MODEL)),
        "ffn2_w_gate": normal(ks[12], (DEPTH, D_MODEL, D_FF), D_MODEL ** -0.5),
        "ffn2_w_up": normal(ks[13], (DEPTH, D_MODEL, D_FF), D_MODEL ** -0.5),
        "ffn2_w_down": normal(ks[14], (DEPTH, D_FF, D_MODEL), D_FF ** -0.5),
        "final_norm": gain(ks[15], (D_MODEL,)),
    }


def reference(x, ffn1_norm, ffn1_w_gate, ffn1_w_up, ffn1_w_down, mix_norm, w_in, sb_out_norm,
              hg_lower_bound_logits, hg_out_norm, w_out, ffn2_norm, ffn2_w_gate, ffn2_w_up,
              ffn2_w_down, final_norm):
    f32 = jnp.float32
    lower_bounds = lax.cumsum(jax.nn.softmax(hg_lower_bound_logits.astype(f32), axis=0), axis=0)

    for l in range(DEPTH):
        h = rmsnorm(x, ffn1_norm[l])
        x = x + 0.5 * swiglu(h, ffn1_w_gate[l], ffn1_w_up[l], ffn1_w_down[l])

        h = rmsnorm(x, mix_norm[l])
        proj = h @ w_in[l]
        q_sb, k_sb, v_sb, q_hg, f_hg, i_hg, g_hg = jnp.split(proj, IN_SPLITS, axis=-1)

        o_sb = stick_breaking_attention(split_heads(q_sb.astype(f32), SB_HEADS, SB_HEAD_DIM),
                                        split_heads(k_sb.astype(f32), SB_HEADS, SB_HEAD_DIM),
                                        split_heads(v_sb.astype(f32), SB_HEADS, SB_HEAD_DIM))
        o_sb = head_rmsnorm(merge_heads(o_sb), sb_out_norm[l], SB_HEADS, SB_HEAD_DIM)

        lb = lower_bounds[l]
        forget = lb + (1.0 - lb) * jax.nn.sigmoid(f_hg.astype(f32))
        q_h = jax.nn.silu(q_hg.astype(f32))
        o_hg = hgrn2_chunkwise(split_heads(q_h, HG_HEADS, HG_HEAD_DIM),
                               split_heads(1.0 - forget, HG_HEADS, HG_HEAD_DIM),
                               split_heads(i_hg.astype(f32), HG_HEADS, HG_HEAD_DIM),
                               split_heads(jnp.log(forget), HG_HEADS, HG_HEAD_DIM))
        o_hg = head_rmsnorm(merge_heads(o_hg), hg_out_norm[l], HG_HEADS, HG_HEAD_DIM) * jax.nn.silu(g_hg.astype(f32))

        mixed = jnp.concatenate([o_sb, o_hg], axis=-1).astype(x.dtype)
        x = x + mixed @ w_out[l]

        h = rmsnorm(x, ffn2_norm[l])
        x = x + 0.5 * swiglu(h, ffn2_w_gate[l], ffn2_w_up[l], ffn2_w_down[l])

    return rmsnorm(x, final_norm)
```

```python
import functools

import jax
import jax.numpy as jnp
import numpy as np
from jax import lax
from jax.experimental import pallas as pl
from jax.experimental.pallas import tpu as pltpu

F32 = jnp.float32
BF16 = jnp.bfloat16

EPS = 1e-6
LANES = 128
SB_HEAD_DIM = 64
SB_WIDTH = 512
HG_HEAD_DIM = 128
HG_WIDTH = 512
V7X_VMEM_LIMIT = 56 * 1024 * 1024

FFN_ROWS = 512
SB_TQ = 256
SB_TK = 256
HG_CHUNK = 128
HG_ROWS = 512


def _dot(a, b):
    return jnp.dot(a, b, preferred_element_type=F32)


def _dot_nt(a, b):
    return lax.dot_general(a, b, (((1,), (1,)), ((), ())), preferred_element_type=F32)


def _dot_tn(a, b):
    return lax.dot_general(a, b, (((0,), (0,)), ((), ())), preferred_element_type=F32)


def _rmsnorm(x, gain):
    return x * lax.rsqrt(jnp.mean(x * x, axis=-1, keepdims=True) + EPS) * gain


def _silu(x):
    return x * (1.0 / (1.0 + jnp.exp(-x)))


def _resident(shape):
    return pl.BlockSpec(shape, lambda *_: (0,) * len(shape), pipeline_mode=pl.Buffered(1))


def _ffn_kernel(*refs, mix_in, final_norm):
    if mix_in:
        x_ref, osb_ref, ohg_ref, wo_ref, *refs = refs
    else:
        x_ref, *refs = refs
    if final_norm:
        gain_ref, wg_ref, wu_ref, wd_ref, fgain_ref, o_ref = refs
    else:
        gain_ref, wg_ref, wu_ref, wd_ref, o_ref = refs
    x = x_ref[...]
    if mix_in:
        x = x + _dot(osb_ref[...], wo_ref[:SB_WIDTH, :]) + _dot(ohg_ref[...], wo_ref[SB_WIDTH:, :])
    h = _rmsnorm(x, gain_ref[...]).astype(BF16)
    g = _dot(h, wg_ref[...])
    u = _dot(h, wu_ref[...])
    a = (_silu(g) * u).astype(BF16)
    y = x + 0.5 * _dot(a, wd_ref[...])
    if final_norm:
        y = _rmsnorm(y, fgain_ref[...])
    o_ref[...] = y


def _ffn(x, gain, wg, wu, wd, *, mix=None, final_gain=None):
    n, d = x.shape
    f = wg.shape[1]
    rows = pl.BlockSpec((FFN_ROWS, d), lambda i: (i, 0))
    args, specs = [x], [rows]
    if mix is not None:
        osb, ohg, wo = mix
        args += [osb, ohg, wo]
        specs += [pl.BlockSpec((FFN_ROWS, SB_WIDTH), lambda i: (i, 0)),
                  pl.BlockSpec((FFN_ROWS, HG_WIDTH), lambda i: (i, 0)),
                  _resident(wo.shape)]
    args += [gain.reshape(1, d), wg, wu, wd]
    specs += [_resident((1, d)), _resident((d, f)), _resident((d, f)), _resident((f, d))]
    if final_gain is not None:
        args.append(final_gain.reshape(1, d))
        specs.append(_resident((1, d)))
    return pl.pallas_call(
        functools.partial(_ffn_kernel, mix_in=mix is not None, final_norm=final_gain is not None),
        grid=(n // FFN_ROWS,),
        in_specs=specs,
        out_specs=rows,
        out_shape=jax.ShapeDtypeStruct((n, d), F32),
        compiler_params=pltpu.CompilerParams(dimension_semantics=("parallel",),
                                             vmem_limit_bytes=V7X_VMEM_LIMIT),
        name="ffn_mix" if mix is not None else "ffn",
    )(*args)


def _proj_kernel(x_ref, gain_ref, w_ref, sb_ref, hg_ref):
    h = _rmsnorm(x_ref[...], gain_ref[...]).astype(BF16)
    p = _dot(h, w_ref[...])
    sb_cols = sb_ref.shape[1]
    sb_ref[...] = p[:, :sb_cols].astype(BF16)
    hg_ref[...] = p[:, sb_cols:]


def _proj(x, gain, w_in):
    n, d = x.shape
    sb_cols, hg_cols = 3 * SB_WIDTH, 4 * HG_WIDTH
    return pl.pallas_call(
        _proj_kernel,
        grid=(n // FFN_ROWS,),
        in_specs=[pl.BlockSpec((FFN_ROWS, d), lambda i: (i, 0)), _resident((1, d)), _resident(w_in.shape)],
        out_specs=[pl.BlockSpec((FFN_ROWS, sb_cols), lambda i: (i, 0)),
                   pl.BlockSpec((FFN_ROWS, hg_cols), lambda i: (i, 0))],
        out_shape=[jax.ShapeDtypeStruct((n, sb_cols), BF16), jax.ShapeDtypeStruct((n, hg_cols), F32)],
        compiler_params=pltpu.CompilerParams(dimension_semantics=("parallel",),
                                             vmem_limit_bytes=V7X_VMEM_LIMIT),
        name="proj",
    )(x, gain.reshape(1, d), w_in)


def _sb_kernel(q_ref, k_ref, v_ref, gain_ref, o_ref):
    i = pl.program_id(2)
    lane = lax.broadcasted_iota(jnp.int32, (1, LANES), 1)
    row = lax.broadcasted_iota(jnp.int32, (SB_TK, SB_TK), 0)
    col = lax.broadcasted_iota(jnp.int32, (SB_TK, SB_TK), 1)
    strict_upper_sum = jnp.where(row > col, 1.0, 0.0).astype(BF16)
    causal = col < row
    q = q_ref[0]
    scale = SB_HEAD_DIM ** -0.5

    def block(qh, kb, vb, carry, acc, diag):
        z = _dot_nt(qh, kb)
        log1p_term = jnp.log(1.0 + jnp.exp(-jnp.abs(z)))
        softplus = jnp.maximum(z, 0.0) + log1p_term
        log_beta = jnp.minimum(z, 0.0) - log1p_term
        if diag:
            softplus = jnp.where(causal, softplus, 0.0)
        tail = _dot(softplus.astype(BF16), strict_upper_sum)
        w = jnp.exp(log_beta - tail - carry)
        if diag:
            w = jnp.where(causal, w, 0.0)
        acc = acc + _dot(w.astype(BF16), vb)
        carry = carry + jnp.sum(softplus, axis=1, keepdims=True)
        return carry, acc

    heads = []
    for h in range(LANES // SB_HEAD_DIM):
        in_head = (lane >= h * SB_HEAD_DIM) & (lane < (h + 1) * SB_HEAD_DIM)
        qh = jnp.where(in_head, q * scale, 0.0).astype(BF16)
        start = pl.multiple_of(i * SB_TQ, SB_TQ)
        carry0 = jnp.zeros((SB_TQ, 1), F32)
        acc0 = jnp.zeros((SB_TQ, LANES), F32)
        carry, acc = block(qh, k_ref[0, pl.ds(start, SB_TK), :], v_ref[0, pl.ds(start, SB_TK), :],
                           carry0, acc0, True)

        def body(jj, state, qh=qh):
            off = pl.multiple_of((i - 1 - jj) * SB_TK, SB_TK)
            return block(qh, k_ref[0, pl.ds(off, SB_TK), :], v_ref[0, pl.ds(off, SB_TK), :], *state, False)

        carry, acc = lax.fori_loop(0, i, body, (carry, acc))
        ms = jnp.sum(jnp.where(in_head, acc * acc, 0.0), axis=1, keepdims=True) * (1.0 / SB_HEAD_DIM)
        heads.append(jnp.where(in_head, acc * lax.rsqrt(ms + EPS), 0.0))
    o_ref[0] = ((heads[0] + heads[1]) * gain_ref[...]).astype(o_ref.dtype)


def _sb_attention(qkv, gain, batch, seq):
    pairs = SB_WIDTH // LANES
    return pl.pallas_call(
        _sb_kernel,
        grid=(batch, pairs, seq // SB_TQ),
        in_specs=[pl.BlockSpec((1, SB_TQ, LANES), lambda b, p, i: (b, i, p)),
                  pl.BlockSpec((1, seq, LANES), lambda b, p, i: (b, 0, pairs + p)),
                  pl.BlockSpec((1, seq, LANES), lambda b, p, i: (b, 0, 2 * pairs + p)),
                  pl.BlockSpec((1, LANES), lambda b, p, i: (0, p))],
        out_specs=pl.BlockSpec((1, SB_TQ, LANES), lambda b, p, i: (b, i, p)),
        out_shape=jax.ShapeDtypeStruct((batch, seq, SB_WIDTH), BF16),
        compiler_params=pltpu.CompilerParams(dimension_semantics=("parallel", "parallel", "parallel"),
                                             vmem_limit_bytes=V7X_VMEM_LIMIT),
        name="sb_attn",
    )(qkv, qkv, qkv, gain.reshape(1, SB_WIDTH))


def _hg_levels():
    return [HG_CHUNK >> (l + 1) for l in range(HG_CHUNK.bit_length() - 1)]


def _hg_constants():
    c = HG_CHUNK
    t = np.arange(c)[:, None]
    j = np.arange(c)[None, :]
    sums = [(j <= t), (j > t)]
    masks = [(t == j)]
    for h in _hg_levels():
        pos = t % (2 * h)
        mid = t - pos + h
        upper = pos >= h
        sums.append(np.where(upper, (j >= mid) & (j <= t), (j > t) & (j < mid)))
        masks.append((t // (2 * h) == j // (2 * h)) & upper & (j % (2 * h) < h))
    return (np.concatenate(sums, axis=0).astype(np.float32), np.stack(masks).astype(np.float32))


def _hg_kernel(q_ref, f_ref, i_ref, g_ref, lbl_ref, gain_ref, sums_ref, masks_ref, o_ref, state_ref):
    @pl.when(pl.program_id(2) == 0)
    def _():
        state_ref[...] = jnp.zeros_like(state_ref)

    c = HG_CHUNK
    logits = lbl_ref[...]
    m = jnp.max(logits, axis=0, keepdims=True)
    e = jnp.exp(logits - m)
    lower = e[0:1, :] / jnp.sum(e, axis=0, keepdims=True)
    gain = gain_ref[...]
    n_levels = len(_hg_levels())

    for ci in range(HG_ROWS // c):
        rows = pl.ds(ci * c, c)
        q = _silu(q_ref[0, rows, :])
        forget = lower + (1.0 - lower) * (1.0 / (1.0 + jnp.exp(-f_ref[0, rows, :])))
        k = 1.0 - forget
        v = i_ref[0, rows, :].astype(BF16)
        log_f = jnp.log(forget)
        hi = log_f.astype(BF16)
        lo = (log_f - hi.astype(F32)).astype(BF16)
        sums = _dot(sums_ref[...], jnp.concatenate([hi, lo], axis=1))
        sums = sums[:, :LANES] + sums[:, LANES:]
        b = sums[0:c]
        suffix = sums[c:2 * c]
        scores = masks_ref[0] * _dot_nt(q.astype(BF16), k.astype(BF16))
        for l in range(n_levels):
            decay = jnp.exp(sums[(2 + l) * c:(3 + l) * c])
            scores = scores + masks_ref[1 + l] * _dot_nt((q * decay).astype(BF16), (k * decay).astype(BF16))
        state = state_ref[...]
        o = _dot(scores.astype(BF16), v) + _dot_nt((q * jnp.exp(b)).astype(BF16), state.astype(BF16))
        state_ref[...] = state * jnp.exp(b[c - 1:c, :]) + _dot_tn(v, (k * jnp.exp(suffix)).astype(BF16))
        o = _rmsnorm(o, gain) * _silu(g_ref[0, rows, :])
        o_ref[0, rows, :] = o.astype(o_ref.dtype)


def _hgrn2(hg, lb_logits, gain, batch, seq):
    heads = HG_WIDTH // HG_HEAD_DIM
    sums, masks = _hg_constants()

    def col(k):
        return pl.BlockSpec((1, HG_ROWS, LANES), lambda b, h, t: (b, t, k * heads + h))

    return pl.pallas_call(
        _hg_kernel,
        grid=(batch, heads, seq // HG_ROWS),
        in_specs=[col(0), col(1), col(2), col(3),
                  pl.BlockSpec((2, LANES), lambda b, h, t: (0, h)),
                  pl.BlockSpec((1, LANES), lambda b, h, t: (0, h)),
                  _resident(sums.shape), _resident(masks.shape)],
        out_specs=pl.BlockSpec((1, HG_ROWS, LANES), lambda b, h, t: (b, t, h)),
        out_shape=jax.ShapeDtypeStruct((batch, seq, HG_WIDTH), BF16),
        scratch_shapes=[pltpu.VMEM((HG_HEAD_DIM, HG_HEAD_DIM), F32)],
        compiler_params=pltpu.CompilerParams(dimension_semantics=("parallel", "parallel", "arbitrary"),
                                             vmem_limit_bytes=V7X_VMEM_LIMIT),
        name="hgrn2",
    )(hg, hg, hg, hg, lb_logits, gain.reshape(1, HG_WIDTH), jnp.asarray(sums, BF16), jnp.asarray(masks, F32))


def kernel(x, ffn1_norm, ffn1_w_gate, ffn1_w_up, ffn1_w_down, mix_norm, w_in, sb_out_norm, hg_lower_bound_logits, hg_out_norm, w_out, ffn2_norm, ffn2_w_gate, ffn2_w_up, ffn2_w_down, final_norm):
    batch, seq, d = x.shape
    assert ffn1_norm.shape[0] == 1, "single-layer kernel"
    assert w_in.shape[2] == 3 * SB_WIDTH + 4 * HG_WIDTH and w_out.shape[1] == SB_WIDTH + HG_WIDTH
    assert seq % max(SB_TQ, HG_ROWS) == 0 and (batch * seq) % FFN_ROWS == 0 and SB_TQ == SB_TK
    n = batch * seq
    bf = lambda w: w[0].astype(BF16)

    x1 = _ffn(x.reshape(n, d), ffn1_norm[0], bf(ffn1_w_gate), bf(ffn1_w_up), bf(ffn1_w_down))
    sb, hg = _proj(x1, mix_norm[0], bf(w_in))
    o_sb = _sb_attention(sb.reshape(batch, seq, -1), sb_out_norm[0], batch, seq)
    o_hg = _hgrn2(hg.reshape(batch, seq, -1), hg_lower_bound_logits, hg_out_norm[0], batch, seq)
    out = _ffn(x1, ffn2_norm[0], bf(ffn2_w_gate), bf(ffn2_w_up), bf(ffn2_w_down),
               mix=(o_sb.reshape(n, -1), o_hg.reshape(n, -1), bf(w_out)), final_gain=final_norm)
    return out.reshape(batch, seq, d)
```

```python
import functools

import jax
import jax.numpy as jnp
import numpy as np
from jax import lax
from jax.experimental import pallas as pl
from jax.experimental.pallas import tpu as pltpu

F32 = jnp.float32
BF16 = jnp.bfloat16

EPS = 1e-6
LANES = 128
SB_HEAD_DIM = 64
SB_WIDTH = 512
HG_HEAD_DIM = 128
HG_WIDTH = 512
V7X_VMEM_LIMIT = 56 * 1024 * 1024

FFN_ROWS = 512
SB_TQ = 512
SB_TK = 256
SB_Q_SCALE = float(np.log2(np.e)) * SB_HEAD_DIM ** -0.5
HG_CHUNK = 128
HG_ROWS = 512


def _dot(a, b):
    return jnp.dot(a, b, preferred_element_type=F32)


def _dot_nt(a, b):
    return lax.dot_general(a, b, (((1,), (1,)), ((), ())), preferred_element_type=F32)


def _dot_tn(a, b):
    return lax.dot_general(a, b, (((0,), (0,)), ((), ())), preferred_element_type=F32)


def _rmsnorm(x, gain):
    return x * lax.rsqrt(jnp.mean(x * x, axis=-1, keepdims=True) + EPS) * gain


def _silu(x):
    return x * (1.0 / (1.0 + jnp.exp(-x)))


def _resident(shape):
    return pl.BlockSpec(shape, lambda *_: (0,) * len(shape), pipeline_mode=pl.Buffered(1))


def _ffn_kernel(*refs, mix_in, final_norm):
    if mix_in:
        x_ref, osb_ref, ohg_ref, wo_ref, *refs = refs
    else:
        x_ref, *refs = refs
    if final_norm:
        gain_ref, wg_ref, wu_ref, wd_ref, fgain_ref, o_ref = refs
    else:
        gain_ref, wg_ref, wu_ref, wd_ref, o_ref = refs
    x = x_ref[...]
    if mix_in:
        x = x + _dot(osb_ref[...], wo_ref[:SB_WIDTH, :]) + _dot(ohg_ref[...], wo_ref[SB_WIDTH:, :])
    h = _rmsnorm(x, gain_ref[...]).astype(BF16)
    g = _dot(h, wg_ref[...])
    u = _dot(h, wu_ref[...])
    a = (_silu(g) * u).astype(BF16)
    y = x + 0.5 * _dot(a, wd_ref[...])
    if final_norm:
        y = _rmsnorm(y, fgain_ref[...])
    o_ref[...] = y


def _ffn(x, gain, wg, wu, wd, *, mix=None, final_gain=None):
    n, d = x.shape
    f = wg.shape[1]
    rows = pl.BlockSpec((FFN_ROWS, d), lambda i: (i, 0))
    args, specs = [x], [rows]
    if mix is not None:
        osb, ohg, wo = mix
        args += [osb, ohg, wo]
        specs += [pl.BlockSpec((FFN_ROWS, SB_WIDTH), lambda i: (i, 0)),
                  pl.BlockSpec((FFN_ROWS, HG_WIDTH), lambda i: (i, 0)),
                  _resident(wo.shape)]
    args += [gain.reshape(1, d), wg, wu, wd]
    specs += [_resident((1, d)), _resident((d, f)), _resident((d, f)), _resident((f, d))]
    if final_gain is not None:
        args.append(final_gain.reshape(1, d))
        specs.append(_resident((1, d)))
    return pl.pallas_call(
        functools.partial(_ffn_kernel, mix_in=mix is not None, final_norm=final_gain is not None),
        grid=(n // FFN_ROWS,),
        in_specs=specs,
        out_specs=rows,
        out_shape=jax.ShapeDtypeStruct((n, d), F32),
        compiler_params=pltpu.CompilerParams(dimension_semantics=("parallel",),
                                             vmem_limit_bytes=V7X_VMEM_LIMIT),
        name="ffn_mix" if mix is not None else "ffn",
    )(*args)


def _proj_kernel(x_ref, gain_ref, w_ref, sb_ref, hg_ref):
    h = _rmsnorm(x_ref[...], gain_ref[...]).astype(BF16)
    p = _dot(h, w_ref[...])
    sb_cols = sb_ref.shape[1]
    sb_ref[:, :SB_WIDTH] = (p[:, :SB_WIDTH] * SB_Q_SCALE).astype(BF16)
    sb_ref[:, SB_WIDTH:] = p[:, SB_WIDTH:sb_cols].astype(BF16)
    hg_ref[...] = p[:, sb_cols:]


def _proj(x, gain, w_in):
    n, d = x.shape
    sb_cols, hg_cols = 3 * SB_WIDTH, 4 * HG_WIDTH
    return pl.pallas_call(
        _proj_kernel,
        grid=(n // FFN_ROWS,),
        in_specs=[pl.BlockSpec((FFN_ROWS, d), lambda i: (i, 0)), _resident((1, d)), _resident(w_in.shape)],
        out_specs=[pl.BlockSpec((FFN_ROWS, sb_cols), lambda i: (i, 0)),
                   pl.BlockSpec((FFN_ROWS, hg_cols), lambda i: (i, 0))],
        out_shape=[jax.ShapeDtypeStruct((n, sb_cols), BF16), jax.ShapeDtypeStruct((n, hg_cols), F32)],
        compiler_params=pltpu.CompilerParams(dimension_semantics=("parallel",),
                                             vmem_limit_bytes=V7X_VMEM_LIMIT),
        name="proj",
    )(x, gain.reshape(1, d), w_in)


def _sb_kernel(q_ref, k_ref, v_ref, gain_ref, o_ref):
    i = pl.program_id(2)
    half = SB_TQ // 2
    lane = lax.broadcasted_iota(jnp.int32, (1, LANES), 1)
    row = lax.broadcasted_iota(jnp.int32, (SB_TK, SB_TK), 0)
    col = lax.broadcasted_iota(jnp.int32, (SB_TK, SB_TK), 1)
    strict_upper_sum = jnp.where(row > col, 1.0, 0.0).astype(BF16)
    causal = col < row
    q = q_ref[0]
    n_heads = LANES // SB_HEAD_DIM
    in_head = [(lane >= h * SB_HEAD_DIM) & (lane < (h + 1) * SB_HEAD_DIM) for h in range(n_heads)]
    qh = [jnp.where(m, q, 0.0).astype(BF16) for m in in_head]

    def block(qrows, kb, vb, carry, acc, mask):
        z = _dot_nt(qrows, kb)
        neg, pos = jnp.minimum(z, 0.0), jnp.maximum(z, 0.0)
        log1p_term = jnp.log2(1.0 + jnp.exp2(neg - pos))
        softplus = pos + log1p_term
        log_beta = neg - log1p_term
        if mask is not None:
            softplus = jnp.where(mask, softplus, 0.0)
        tail = _dot(softplus.astype(BF16), strict_upper_sum)
        w = jnp.exp2(log_beta - tail - carry)
        if mask is not None:
            w = jnp.where(mask, w, 0.0)
        acc = acc + _dot(w.astype(BF16), vb)
        carry = carry + jnp.sum(softplus, axis=1, keepdims=True)
        return carry, acc

    def keys(kb):
        rows = pl.ds(pl.multiple_of(kb * SB_TK, SB_TK), SB_TK)
        return k_ref[0, rows, :], v_ref[0, rows, :]

    k_lo, v_lo = keys(2 * i)
    k_hi, v_hi = keys(2 * i + 1)
    state = []
    for h in range(n_heads):
        zero = (jnp.zeros((half, 1), F32), jnp.zeros((half, LANES), F32))
        carry_u, acc_u = block(qh[h][:half], k_lo, v_lo, *zero, causal)
        carry_l, acc_l = block(qh[h][half:], k_hi, v_hi, *zero, causal)
        carry_l, acc_l = block(qh[h][half:], k_lo, v_lo, carry_l, acc_l, None)
        state += [jnp.concatenate([carry_u, carry_l], axis=0), jnp.concatenate([acc_u, acc_l], axis=0)]

    def body(jj, state):
        kb, vb = keys(2 * i - 1 - jj)
        out = []
        for h in range(n_heads):
            out += block(qh[h], kb, vb, state[2 * h], state[2 * h + 1], None)
        return tuple(out)

    state = lax.fori_loop(0, 2 * i, body, tuple(state))
    out = jnp.zeros((SB_TQ, LANES), F32)
    for h in range(n_heads):
        acc = state[2 * h + 1]
        ms = jnp.sum(jnp.where(in_head[h], acc * acc, 0.0), axis=1, keepdims=True) * (1.0 / SB_HEAD_DIM)
        out = out + jnp.where(in_head[h], acc * lax.rsqrt(ms + EPS), 0.0)
    o_ref[0] = (out * gain_ref[...]).astype(o_ref.dtype)


def _sb_attention(qkv, gain, batch, seq):
    pairs = SB_WIDTH // LANES
    return pl.pallas_call(
        _sb_kernel,
        grid=(batch, pairs, seq // SB_TQ),
        in_specs=[pl.BlockSpec((1, SB_TQ, LANES), lambda b, p, i: (b, i, p)),
                  pl.BlockSpec((1, seq, LANES), lambda b, p, i: (b, 0, pairs + p)),
                  pl.BlockSpec((1, seq, LANES), lambda b, p, i: (b, 0, 2 * pairs + p)),
                  pl.BlockSpec((1, LANES), lambda b, p, i: (0, p))],
        out_specs=pl.BlockSpec((1, SB_TQ, LANES), lambda b, p, i: (b, i, p)),
        out_shape=jax.ShapeDtypeStruct((batch, seq, SB_WIDTH), BF16),
        compiler_params=pltpu.CompilerParams(dimension_semantics=("parallel", "parallel", "parallel"),
                                             vmem_limit_bytes=V7X_VMEM_LIMIT),
        name="sb_attn",
    )(qkv, qkv, qkv, gain.reshape(1, SB_WIDTH))


def _hg_levels():
    return [HG_CHUNK >> (l + 1) for l in range(HG_CHUNK.bit_length() - 1)]


def _hg_constants():
    c = HG_CHUNK
    t = np.arange(c)[:, None]
    j = np.arange(c)[None, :]
    sums = [(j <= t), (j > t)]
    masks = [(t == j)]
    for h in _hg_levels():
        pos = t % (2 * h)
        mid = t - pos + h
        upper = pos >= h
        sums.append(np.where(upper, (j >= mid) & (j <= t), (j > t) & (j < mid)))
        masks.append((t // (2 * h) == j // (2 * h)) & upper & (j % (2 * h) < h))
    return (np.concatenate(sums, axis=0).astype(np.float32), np.stack(masks).astype(np.float32))


def _hg_kernel(q_ref, f_ref, i_ref, g_ref, lbl_ref, gain_ref, sums_ref, masks_ref, o_ref, state_ref):
    @pl.when(pl.program_id(2) == 0)
    def _():
        state_ref[...] = jnp.zeros_like(state_ref)

    c = HG_CHUNK
    logits = lbl_ref[...]
    m = jnp.max(logits, axis=0, keepdims=True)
    e = jnp.exp(logits - m)
    lower = e[0:1, :] / jnp.sum(e, axis=0, keepdims=True)
    gain = gain_ref[...]
    n_levels = len(_hg_levels())

    for ci in range(HG_ROWS // c):
        rows = pl.ds(ci * c, c)
        q = _silu(q_ref[0, rows, :])
        forget = lower + (1.0 - lower) * (1.0 / (1.0 + jnp.exp(-f_ref[0, rows, :])))
        k = 1.0 - forget
        v = i_ref[0, rows, :].astype(BF16)
        log_f = jnp.log(forget)
        hi = log_f.astype(BF16)
        lo = (log_f - hi.astype(F32)).astype(BF16)
        sums = _dot(sums_ref[...], jnp.concatenate([hi, lo], axis=1))
        sums = sums[:, :LANES] + sums[:, LANES:]
        b = sums[0:c]
        suffix = sums[c:2 * c]
        scores = masks_ref[0] * _dot_nt(q.astype(BF16), k.astype(BF16))
        for l in range(n_levels):
            decay = jnp.exp(sums[(2 + l) * c:(3 + l) * c])
            scores = scores + masks_ref[1 + l] * _dot_nt((q * decay).astype(BF16), (k * decay).astype(BF16))
        state = state_ref[...]
        o = _dot(scores.astype(BF16), v) + _dot_nt((q * jnp.exp(b)).astype(BF16), state.astype(BF16))
        state_ref[...] = state * jnp.exp(b[c - 1:c, :]) + _dot_tn(v, (k * jnp.exp(suffix)).astype(BF16))
        o = _rmsnorm(o, gain) * _silu(g_ref[0, rows, :])
        o_ref[0, rows, :] = o.astype(o_ref.dtype)


def _hgrn2(hg, lb_logits, gain, batch, seq):
    heads = HG_WIDTH // HG_HEAD_DIM
    sums, masks = _hg_constants()

    def col(k):
        return pl.BlockSpec((1, HG_ROWS, LANES), lambda b, h, t: (b, t, k * heads + h))

    return pl.pallas_call(
        _hg_kernel,
        grid=(batch, heads, seq // HG_ROWS),
        in_specs=[col(0), col(1), col(2), col(3),
                  pl.BlockSpec((2, LANES), lambda b, h, t: (0, h)),
                  pl.BlockSpec((1, LANES), lambda b, h, t: (0, h)),
                  _resident(sums.shape), _resident(masks.shape)],
        out_specs=pl.BlockSpec((1, HG_ROWS, LANES), lambda b, h, t: (b, t, h)),
        out_shape=jax.ShapeDtypeStruct((batch, seq, HG_WIDTH), BF16),
        scratch_shapes=[pltpu.VMEM((HG_HEAD_DIM, HG_HEAD_DIM), F32)],
        compiler_params=pltpu.CompilerParams(dimension_semantics=("parallel", "parallel", "arbitrary"),
                                             vmem_limit_bytes=V7X_VMEM_LIMIT),
        name="hgrn2",
    )(hg, hg, hg, hg, lb_logits, gain.reshape(1, HG_WIDTH), jnp.asarray(sums, BF16), jnp.asarray(masks, F32))


def kernel(x, ffn1_norm, ffn1_w_gate, ffn1_w_up, ffn1_w_down, mix_norm, w_in, sb_out_norm, hg_lower_bound_logits, hg_out_norm, w_out, ffn2_norm, ffn2_w_gate, ffn2_w_up, ffn2_w_down, final_norm):
    batch, seq, d = x.shape
    assert ffn1_norm.shape[0] == 1, "single-layer kernel"
    assert w_in.shape[2] == 3 * SB_WIDTH + 4 * HG_WIDTH and w_out.shape[1] == SB_WIDTH + HG_WIDTH
    assert seq % max(SB_TQ, HG_ROWS) == 0 and (batch * seq) % FFN_ROWS == 0 and SB_TQ == 2 * SB_TK
    n = batch * seq
    bf = lambda w: w[0].astype(BF16)

    x1 = _ffn(x.reshape(n, d), ffn1_norm[0], bf(ffn1_w_gate), bf(ffn1_w_up), bf(ffn1_w_down))
    sb, hg = _proj(x1, mix_norm[0], bf(w_in))
    o_sb = _sb_attention(sb.reshape(batch, seq, -1), sb_out_norm[0], batch, seq)
    o_hg = _hgrn2(hg.reshape(batch, seq, -1), hg_lower_bound_logits, hg_out_norm[0], batch, seq)
    out = _ffn(x1, ffn2_norm[0], bf(ffn2_w_gate), bf(ffn2_w_up), bf(ffn2_w_down),
               mix=(o_sb.reshape(n, -1), o_hg.reshape(n, -1), bf(w_out)), final_gain=final_norm)
    return out.reshape(batch, seq, d)
```

```python
import functools

import jax
import jax.numpy as jnp
import numpy as np
from jax import lax
from jax.experimental import pallas as pl
from jax.experimental.pallas import tpu as pltpu

F32 = jnp.float32
BF16 = jnp.bfloat16

EPS = 1e-6
LANES = 128
SB_HEAD_DIM = 64
SB_WIDTH = 512
HG_HEAD_DIM = 128
HG_WIDTH = 512
V7X_VMEM_LIMIT = 56 * 1024 * 1024

FFN_ROWS = 512
SB_TQ = 512
SB_TK = 256
SB_Q_SCALE = float(np.log2(np.e)) * SB_HEAD_DIM ** -0.5
HG_CHUNK = 128
HG_ROWS = 512


def _dot(a, b):
    return jnp.dot(a, b, preferred_element_type=F32)


def _dot_nt(a, b):
    return lax.dot_general(a, b, (((1,), (1,)), ((), ())), preferred_element_type=F32)


def _dot_tn(a, b):
    return lax.dot_general(a, b, (((0,), (0,)), ((), ())), preferred_element_type=F32)


def _rmsnorm(x, gain):
    return x * lax.rsqrt(jnp.mean(x * x, axis=-1, keepdims=True) + EPS) * gain


def _silu(x):
    return x * (1.0 / (1.0 + jnp.exp(-x)))


def _resident(shape):
    return pl.BlockSpec(shape, lambda *_: (0,) * len(shape), pipeline_mode=pl.Buffered(1))


def _ffn_kernel(*refs, mix_in, final_norm):
    if mix_in:
        x_ref, osb_ref, ohg_ref, wo_ref, *refs = refs
    else:
        x_ref, *refs = refs
    if final_norm:
        gain_ref, wg_ref, wu_ref, wd_ref, fgain_ref, o_ref = refs
    else:
        gain_ref, wg_ref, wu_ref, wd_ref, o_ref = refs
    x = x_ref[...]
    if mix_in:
        x = x + _dot(osb_ref[...], wo_ref[:SB_WIDTH, :]) + _dot(ohg_ref[...], wo_ref[SB_WIDTH:, :])
    h = _rmsnorm(x, gain_ref[...]).astype(BF16)
    g = _dot(h, wg_ref[...])
    u = _dot(h, wu_ref[...])
    a = (_silu(g) * u).astype(BF16)
    y = x + 0.5 * _dot(a, wd_ref[...])
    if final_norm:
        y = _rmsnorm(y, fgain_ref[...])
    o_ref[...] = y


def _ffn(x, gain, wg, wu, wd, *, mix=None, final_gain=None):
    n, d = x.shape
    f = wg.shape[1]
    rows = pl.BlockSpec((FFN_ROWS, d), lambda i: (i, 0))
    args, specs = [x], [rows]
    if mix is not None:
        osb, ohg, wo = mix
        args += [osb, ohg, wo]
        specs += [pl.BlockSpec((FFN_ROWS, SB_WIDTH), lambda i: (i, 0)),
                  pl.BlockSpec((FFN_ROWS, HG_WIDTH), lambda i: (i, 0)),
                  _resident(wo.shape)]
    args += [gain.reshape(1, d), wg, wu, wd]
    specs += [_resident((1, d)), _resident((d, f)), _resident((d, f)), _resident((f, d))]
    if final_gain is not None:
        args.append(final_gain.reshape(1, d))
        specs.append(_resident((1, d)))
    return pl.pallas_call(
        functools.partial(_ffn_kernel, mix_in=mix is not None, final_norm=final_gain is not None),
        grid=(n // FFN_ROWS,),
        in_specs=specs,
        out_specs=rows,
        out_shape=jax.ShapeDtypeStruct((n, d), F32),
        compiler_params=pltpu.CompilerParams(dimension_semantics=("parallel",),
                                             vmem_limit_bytes=V7X_VMEM_LIMIT),
        name="ffn_mix" if mix is not None else "ffn",
    )(*args)


def _proj_kernel(x_ref, gain_ref, w_ref, sb_ref, hg_ref):
    h = _rmsnorm(x_ref[...], gain_ref[...]).astype(BF16)
    p = _dot(h, w_ref[...])
    sb_cols = sb_ref.shape[1]
    sb_ref[:, :SB_WIDTH] = (p[:, :SB_WIDTH] * SB_Q_SCALE).astype(BF16)
    sb_ref[:, SB_WIDTH:] = p[:, SB_WIDTH:sb_cols].astype(BF16)
    hg_ref[...] = p[:, sb_cols:]


def _proj(x, gain, w_in):
    n, d = x.shape
    sb_cols, hg_cols = 3 * SB_WIDTH, 4 * HG_WIDTH
    return pl.pallas_call(
        _proj_kernel,
        grid=(n // FFN_ROWS,),
        in_specs=[pl.BlockSpec((FFN_ROWS, d), lambda i: (i, 0)), _resident((1, d)), _resident(w_in.shape)],
        out_specs=[pl.BlockSpec((FFN_ROWS, sb_cols), lambda i: (i, 0)),
                   pl.BlockSpec((FFN_ROWS, hg_cols), lambda i: (i, 0))],
        out_shape=[jax.ShapeDtypeStruct((n, sb_cols), BF16), jax.ShapeDtypeStruct((n, hg_cols), F32)],
        compiler_params=pltpu.CompilerParams(dimension_semantics=("parallel",),
                                             vmem_limit_bytes=V7X_VMEM_LIMIT),
        name="proj",
    )(x, gain.reshape(1, d), w_in)


def _sb_kernel(q_ref, k_ref, v_ref, gain_ref, tri_ref, o_ref, sp_buf, lb_buf, rs_buf, carry_ref, acc_ref):
    i = pl.program_id(2)
    half = SB_TQ // 2
    lane = lax.broadcasted_iota(jnp.int32, (1, LANES), 1)
    row = lax.broadcasted_iota(jnp.int32, (SB_TK, SB_TK), 0)
    col = lax.broadcasted_iota(jnp.int32, (SB_TK, SB_TK), 1)
    causal = col < row
    q = q_ref[0]
    n_heads = LANES // SB_HEAD_DIM
    in_head = [(lane >= h * SB_HEAD_DIM) & (lane < (h + 1) * SB_HEAD_DIM) for h in range(n_heads)]
    qh = [jnp.where(m, q, 0.0).astype(BF16) for m in in_head]

    def logits(qrows, kb, mask):
        z = _dot_nt(qrows, kb)
        log1p_term = jnp.log2(1.0 + jnp.exp2(-jnp.abs(z)))
        softplus = jnp.maximum(z + log1p_term, log1p_term)
        log_beta = z - softplus
        if mask is not None:
            softplus = jnp.where(mask, softplus, 0.0)
        return softplus, log_beta

    def weights(softplus_bf16, log_beta, carry, mask):
        tail = _dot(softplus_bf16, tri_ref[...])
        w = jnp.exp2(log_beta - tail - carry)
        if mask is not None:
            w = jnp.where(mask, w, 0.0)
        return w.astype(BF16)

    def block(qrows, kb, vb, carry, acc, mask):
        softplus, log_beta = logits(qrows, kb, mask)
        w = weights(softplus.astype(BF16), log_beta, carry, mask)
        return carry + jnp.sum(softplus, axis=1, keepdims=True), acc + _dot(w, vb)

    def rows_of(kb):
        return pl.ds(pl.multiple_of(kb * SB_TK, SB_TK), SB_TK)

    k_lo, v_lo = k_ref[0, rows_of(2 * i), :], v_ref[0, rows_of(2 * i), :]
    k_hi, v_hi = k_ref[0, rows_of(2 * i + 1), :], v_ref[0, rows_of(2 * i + 1), :]
    for h in range(n_heads):
        zero = (jnp.zeros((half, 1), F32), jnp.zeros((half, LANES), F32))
        carry_u, acc_u = block(qh[h][:half], k_lo, v_lo, *zero, causal)
        carry_l, acc_l = block(qh[h][half:], k_hi, v_hi, *zero, causal)
        carry_l, acc_l = block(qh[h][half:], k_lo, v_lo, carry_l, acc_l, None)
        carry_ref[h] = jnp.concatenate([carry_u, carry_l], axis=0)
        acc_ref[h] = jnp.concatenate([acc_u, acc_l], axis=0)

    def stage_logits(slot, kb):
        k_blk = k_ref[0, rows_of(kb), :]
        for h in range(n_heads):
            softplus, log_beta = logits(qh[h], k_blk, None)
            sp_buf[slot, h] = softplus.astype(BF16)
            lb_buf[slot, h] = log_beta
            rs_buf[slot, h] = jnp.sum(softplus, axis=1, keepdims=True)

    def stage_values(slot, kb):
        v_blk = v_ref[0, rows_of(kb), :]
        for h in range(n_heads):
            carry = carry_ref[h]
            w = weights(sp_buf[slot, h], lb_buf[slot, h], carry, None)
            acc_ref[h] += _dot(w, v_blk)
            carry_ref[h] = carry + rs_buf[slot, h]

    @pl.when(i > 0)
    def _():
        top = 2 * i - 1
        stage_logits(0, top)

        def body(p, _):
            b0 = top - 2 * p
            stage_logits(1, b0 - 1)
            stage_values(0, b0)
            stage_logits(0, b0 - 2)
            stage_values(1, b0 - 1)
            return 0

        lax.fori_loop(0, i - 1, body, 0)
        stage_logits(1, 0)
        stage_values(0, 1)
        stage_values(1, 0)

    out = jnp.zeros((SB_TQ, LANES), F32)
    for h in range(n_heads):
        acc = acc_ref[h]
        ms = jnp.sum(jnp.where(in_head[h], acc * acc, 0.0), axis=1, keepdims=True) * (1.0 / SB_HEAD_DIM)
        out = out + jnp.where(in_head[h], acc * lax.rsqrt(ms + EPS), 0.0)
    o_ref[0] = (out * gain_ref[...]).astype(o_ref.dtype)


def _sb_attention(qkv, gain, batch, seq):
    pairs = SB_WIDTH // LANES
    n_heads = LANES // SB_HEAD_DIM
    tri = np.tril(np.ones((SB_TK, SB_TK), np.float32), -1)
    return pl.pallas_call(
        _sb_kernel,
        grid=(batch, pairs, seq // SB_TQ),
        in_specs=[pl.BlockSpec((1, SB_TQ, LANES), lambda b, p, i: (b, i, p)),
                  pl.BlockSpec((1, seq, LANES), lambda b, p, i: (b, 0, pairs + p)),
                  pl.BlockSpec((1, seq, LANES), lambda b, p, i: (b, 0, 2 * pairs + p)),
                  pl.BlockSpec((1, LANES), lambda b, p, i: (0, p)),
                  _resident((SB_TK, SB_TK))],
        out_specs=pl.BlockSpec((1, SB_TQ, LANES), lambda b, p, i: (b, i, p)),
        out_shape=jax.ShapeDtypeStruct((batch, seq, SB_WIDTH), BF16),
        scratch_shapes=[pltpu.VMEM((2, n_heads, SB_TQ, SB_TK), BF16),
                        pltpu.VMEM((2, n_heads, SB_TQ, SB_TK), F32),
                        pltpu.VMEM((2, n_heads, SB_TQ, 1), F32),
                        pltpu.VMEM((n_heads, SB_TQ, 1), F32),
                        pltpu.VMEM((n_heads, SB_TQ, LANES), F32)],
        compiler_params=pltpu.CompilerParams(dimension_semantics=("parallel", "parallel", "parallel"),
                                             vmem_limit_bytes=V7X_VMEM_LIMIT),
        name="sb_attn",
    )(qkv, qkv, qkv, gain.reshape(1, SB_WIDTH), jnp.asarray(tri, BF16))


def _hg_levels():
    return [HG_CHUNK >> (l + 1) for l in range(HG_CHUNK.bit_length() - 1)]


def _hg_constants():
    c = HG_CHUNK
    t = np.arange(c)[:, None]
    j = np.arange(c)[None, :]
    sums = [(j <= t), (j > t)]
    masks = [(t == j)]
    for h in _hg_levels():
        pos = t % (2 * h)
        mid = t - pos + h
        upper = pos >= h
        sums.append(np.where(upper, (j >= mid) & (j <= t), (j > t) & (j < mid)))
        masks.append((t // (2 * h) == j // (2 * h)) & upper & (j % (2 * h) < h))
    return (np.concatenate(sums, axis=0).astype(np.float32), np.stack(masks).astype(np.float32))


def _hg_kernel(q_ref, f_ref, i_ref, g_ref, lbl_ref, gain_ref, sums_ref, masks_ref, o_ref, state_ref):
    @pl.when(pl.program_id(2) == 0)
    def _():
        state_ref[...] = jnp.zeros_like(state_ref)

    c = HG_CHUNK
    logits = lbl_ref[...]
    m = jnp.max(logits, axis=0, keepdims=True)
    e = jnp.exp(logits - m)
    lower = e[0:1, :] / jnp.sum(e, axis=0, keepdims=True)
    gain = gain_ref[...]
    n_levels = len(_hg_levels())

    for ci in range(HG_ROWS // c):
        rows = pl.ds(ci * c, c)
        q = _silu(q_ref[0, rows, :])
        forget = lower + (1.0 - lower) * (1.0 / (1.0 + jnp.exp(-f_ref[0, rows, :])))
        k = 1.0 - forget
        v = i_ref[0, rows, :].astype(BF16)
        log_f = jnp.log(forget)
        hi = log_f.astype(BF16)
        lo = (log_f - hi.astype(F32)).astype(BF16)
        sums = _dot(sums_ref[...], jnp.concatenate([hi, lo], axis=1))
        sums = sums[:, :LANES] + sums[:, LANES:]
        b = sums[0:c]
        suffix = sums[c:2 * c]
        scores = masks_ref[0] * _dot_nt(q.astype(BF16), k.astype(BF16))
        for l in range(n_levels):
            decay = jnp.exp(sums[(2 + l) * c:(3 + l) * c])
            scores = scores + masks_ref[1 + l] * _dot_nt((q * decay).astype(BF16), (k * decay).astype(BF16))
        state = state_ref[...]
        o = _dot(scores.astype(BF16), v) + _dot_nt((q * jnp.exp(b)).astype(BF16), state.astype(BF16))
        state_ref[...] = state * jnp.exp(b[c - 1:c, :]) + _dot_tn(v, (k * jnp.exp(suffix)).astype(BF16))
        o = _rmsnorm(o, gain) * _silu(g_ref[0, rows, :])
        o_ref[0, rows, :] = o.astype(o_ref.dtype)


def _hgrn2(hg, lb_logits, gain, batch, seq):
    heads = HG_WIDTH // HG_HEAD_DIM
    sums, masks = _hg_constants()

    def col(k):
        return pl.BlockSpec((1, HG_ROWS, LANES), lambda b, h, t: (b, t, k * heads + h))

    return pl.pallas_call(
        _hg_kernel,
        grid=(batch, heads, seq // HG_ROWS),
        in_specs=[col(0), col(1), col(2), col(3),
                  pl.BlockSpec((2, LANES), lambda b, h, t: (0, h)),
                  pl.BlockSpec((1, LANES), lambda b, h, t: (0, h)),
                  _resident(sums.shape), _resident(masks.shape)],
        out_specs=pl.BlockSpec((1, HG_ROWS, LANES), lambda b, h, t: (b, t, h)),
        out_shape=jax.ShapeDtypeStruct((batch, seq, HG_WIDTH), BF16),
        scratch_shapes=[pltpu.VMEM((HG_HEAD_DIM, HG_HEAD_DIM), F32)],
        compiler_params=pltpu.CompilerParams(dimension_semantics=("parallel", "parallel", "arbitrary"),
                                             vmem_limit_bytes=V7X_VMEM_LIMIT),
        name="hgrn2",
    )(hg, hg, hg, hg, lb_logits, gain.reshape(1, HG_WIDTH), jnp.asarray(sums, BF16), jnp.asarray(masks, F32))


def kernel(x, ffn1_norm, ffn1_w_gate, ffn1_w_up, ffn1_w_down, mix_norm, w_in, sb_out_norm, hg_lower_bound_logits, hg_out_norm, w_out, ffn2_norm, ffn2_w_gate, ffn2_w_up, ffn2_w_down, final_norm):
    batch, seq, d = x.shape
    assert ffn1_norm.shape[0] == 1, "single-layer kernel"
    assert w_in.shape[2] == 3 * SB_WIDTH + 4 * HG_WIDTH and w_out.shape[1] == SB_WIDTH + HG_WIDTH
    assert seq % max(SB_TQ, HG_ROWS) == 0 and (batch * seq) % FFN_ROWS == 0 and SB_TQ == 2 * SB_TK
    n = batch * seq
    bf = lambda w: w[0].astype(BF16)

    x1 = _ffn(x.reshape(n, d), ffn1_norm[0], bf(ffn1_w_gate), bf(ffn1_w_up), bf(ffn1_w_down))
    sb, hg = _proj(x1, mix_norm[0], bf(w_in))
    o_sb = _sb_attention(sb.reshape(batch, seq, -1), sb_out_norm[0], batch, seq)
    o_hg = _hgrn2(hg.reshape(batch, seq, -1), hg_lower_bound_logits, hg_out_norm[0], batch, seq)
    out = _ffn(x1, ffn2_norm[0], bf(ffn2_w_gate), bf(ffn2_w_up), bf(ffn2_w_down),
               mix=(o_sb.reshape(n, -1), o_hg.reshape(n, -1), bf(w_out)), final_gain=final_norm)
    return out.reshape(batch, seq, d)
```

```python
import functools

import jax
import jax.numpy as jnp
import numpy as np
from jax import lax
from jax.experimental import pallas as pl
from jax.experimental.pallas import tpu as pltpu

F32 = jnp.float32
BF16 = jnp.bfloat16

EPS = 1e-6
LANES = 128
SB_HEAD_DIM = 64
SB_WIDTH = 512
HG_HEAD_DIM = 128
HG_WIDTH = 512
V7X_VMEM_LIMIT = 56 * 1024 * 1024

FFN_ROWS = 512
SB_TQ = 512
SB_TK = 256
SB_Q_SCALE = float(np.log2(np.e)) * SB_HEAD_DIM ** -0.5
SB_DEAD_LOG2 = 160.0
HG_CHUNK = 128
HG_ROWS = 512


def _dot(a, b):
    return jnp.dot(a, b, preferred_element_type=F32)


def _dot_nt(a, b):
    return lax.dot_general(a, b, (((1,), (1,)), ((), ())), preferred_element_type=F32)


def _dot_tn(a, b):
    return lax.dot_general(a, b, (((0,), (0,)), ((), ())), preferred_element_type=F32)


def _rmsnorm(x, gain):
    return x * lax.rsqrt(jnp.mean(x * x, axis=-1, keepdims=True) + EPS) * gain


def _silu(x):
    return x * (1.0 / (1.0 + jnp.exp(-x)))


def _resident(shape):
    return pl.BlockSpec(shape, lambda *_: (0,) * len(shape), pipeline_mode=pl.Buffered(1))


def _ffn_kernel(*refs, mix_in, final_norm):
    if mix_in:
        x_ref, osb_ref, ohg_ref, wo_ref, *refs = refs
    else:
        x_ref, *refs = refs
    if final_norm:
        gain_ref, wg_ref, wu_ref, wd_ref, fgain_ref, o_ref = refs
    else:
        gain_ref, wg_ref, wu_ref, wd_ref, o_ref = refs
    x = x_ref[...]
    if mix_in:
        x = x + _dot(osb_ref[...], wo_ref[:SB_WIDTH, :]) + _dot(ohg_ref[...], wo_ref[SB_WIDTH:, :])
    h = _rmsnorm(x, gain_ref[...]).astype(BF16)
    g = _dot(h, wg_ref[...])
    u = _dot(h, wu_ref[...])
    a = (_silu(g) * u).astype(BF16)
    y = x + 0.5 * _dot(a, wd_ref[...])
    if final_norm:
        y = _rmsnorm(y, fgain_ref[...])
    o_ref[...] = y


def _ffn(x, gain, wg, wu, wd, *, mix=None, final_gain=None):
    n, d = x.shape
    f = wg.shape[1]
    rows = pl.BlockSpec((FFN_ROWS, d), lambda i: (i, 0))
    args, specs = [x], [rows]
    if mix is not None:
        osb, ohg, wo = mix
        args += [osb, ohg, wo]
        specs += [pl.BlockSpec((FFN_ROWS, SB_WIDTH), lambda i: (i, 0)),
                  pl.BlockSpec((FFN_ROWS, HG_WIDTH), lambda i: (i, 0)),
                  _resident(wo.shape)]
    args += [gain.reshape(1, d), wg, wu, wd]
    specs += [_resident((1, d)), _resident((d, f)), _resident((d, f)), _resident((f, d))]
    if final_gain is not None:
        args.append(final_gain.reshape(1, d))
        specs.append(_resident((1, d)))
    return pl.pallas_call(
        functools.partial(_ffn_kernel, mix_in=mix is not None, final_norm=final_gain is not None),
        grid=(n // FFN_ROWS,),
        in_specs=specs,
        out_specs=rows,
        out_shape=jax.ShapeDtypeStruct((n, d), F32),
        compiler_params=pltpu.CompilerParams(dimension_semantics=("parallel",),
                                             vmem_limit_bytes=V7X_VMEM_LIMIT),
        name="ffn_mix" if mix is not None else "ffn",
    )(*args)


def _proj_kernel(x_ref, gain_ref, w_ref, sb_ref, hg_ref):
    h = _rmsnorm(x_ref[...], gain_ref[...]).astype(BF16)
    p = _dot(h, w_ref[...])
    sb_cols = sb_ref.shape[1]
    sb_ref[:, :SB_WIDTH] = (p[:, :SB_WIDTH] * SB_Q_SCALE).astype(BF16)
    sb_ref[:, SB_WIDTH:] = p[:, SB_WIDTH:sb_cols].astype(BF16)
    hg_ref[...] = p[:, sb_cols:]


def _proj(x, gain, w_in):
    n, d = x.shape
    sb_cols, hg_cols = 3 * SB_WIDTH, 4 * HG_WIDTH
    return pl.pallas_call(
        _proj_kernel,
        grid=(n // FFN_ROWS,),
        in_specs=[pl.BlockSpec((FFN_ROWS, d), lambda i: (i, 0)), _resident((1, d)), _resident(w_in.shape)],
        out_specs=[pl.BlockSpec((FFN_ROWS, sb_cols), lambda i: (i, 0)),
                   pl.BlockSpec((FFN_ROWS, hg_cols), lambda i: (i, 0))],
        out_shape=[jax.ShapeDtypeStruct((n, sb_cols), BF16), jax.ShapeDtypeStruct((n, hg_cols), F32)],
        compiler_params=pltpu.CompilerParams(dimension_semantics=("parallel",),
                                             vmem_limit_bytes=V7X_VMEM_LIMIT),
        name="proj",
    )(x, gain.reshape(1, d), w_in)


def _sb_kernel(q_ref, k_ref, v_ref, gain_ref, tri_ref, o_ref, sp_buf, lb_buf, rs_buf, carry_ref, acc_ref):
    i = pl.program_id(2)
    half = SB_TQ // 2
    lane = lax.broadcasted_iota(jnp.int32, (1, LANES), 1)
    row = lax.broadcasted_iota(jnp.int32, (SB_TK, SB_TK), 0)
    col = lax.broadcasted_iota(jnp.int32, (SB_TK, SB_TK), 1)
    causal = col < row
    q = q_ref[0]
    n_heads = LANES // SB_HEAD_DIM
    in_head = [(lane >= h * SB_HEAD_DIM) & (lane < (h + 1) * SB_HEAD_DIM) for h in range(n_heads)]
    qh = [jnp.where(m, q, 0.0).astype(BF16) for m in in_head]

    def logits(qrows, kb, mask):
        z = _dot_nt(qrows, kb)
        log1p_term = jnp.log2(1.0 + jnp.exp2(-jnp.abs(z)))
        softplus = jnp.maximum(z + log1p_term, log1p_term)
        log_beta = z - softplus
        if mask is not None:
            softplus = jnp.where(mask, softplus, 0.0)
        return softplus, log_beta

    def weights(softplus_bf16, log_beta, carry, mask):
        tail = _dot(softplus_bf16, tri_ref[...])
        w = jnp.exp2(log_beta - tail - carry)
        if mask is not None:
            w = jnp.where(mask, w, 0.0)
        return w.astype(BF16)

    def block(qrows, kb, vb, carry, acc, mask):
        softplus, log_beta = logits(qrows, kb, mask)
        w = weights(softplus.astype(BF16), log_beta, carry, mask)
        return carry + jnp.sum(softplus, axis=1, keepdims=True), acc + _dot(w, vb)

    def rows_of(kb):
        return pl.ds(pl.multiple_of(kb * SB_TK, SB_TK), SB_TK)

    k_lo, v_lo = k_ref[0, rows_of(2 * i), :], v_ref[0, rows_of(2 * i), :]
    k_hi, v_hi = k_ref[0, rows_of(2 * i + 1), :], v_ref[0, rows_of(2 * i + 1), :]
    for h in range(n_heads):
        zero = (jnp.zeros((half, 1), F32), jnp.zeros((half, LANES), F32))
        carry_u, acc_u = block(qh[h][:half], k_lo, v_lo, *zero, causal)
        carry_l, acc_l = block(qh[h][half:], k_hi, v_hi, *zero, causal)
        carry_l, acc_l = block(qh[h][half:], k_lo, v_lo, carry_l, acc_l, None)
        carry_ref[h] = jnp.concatenate([carry_u, carry_l], axis=0)
        acc_ref[h] = jnp.concatenate([acc_u, acc_l], axis=0)

    def stage_logits(slot, kb):
        k_blk = k_ref[0, rows_of(kb), :]
        for h in range(n_heads):
            softplus, log_beta = logits(qh[h], k_blk, None)
            sp_buf[slot, h] = softplus.astype(BF16)
            lb_buf[slot, h] = log_beta
            rs_buf[slot, h] = jnp.sum(softplus, axis=1, keepdims=True)

    def stage_values(slot, kb):
        v_blk = v_ref[0, rows_of(kb), :]
        for h in range(n_heads):
            carry = carry_ref[h]
            w = weights(sp_buf[slot, h], lb_buf[slot, h], carry, None)
            acc_ref[h] += _dot(w, v_blk)
            carry_ref[h] = carry + rs_buf[slot, h]

    def decay():
        return jnp.min(carry_ref[...])

    top = 2 * i - 1

    @pl.when(i > 0)
    def _():
        stage_logits(0, top)
        stage_values(0, top)

        @pl.when(decay() < SB_DEAD_LOG2)
        def _():
            stage_logits(1, top - 1)
            stage_values(1, top - 1)

    @pl.when((i > 1) & (decay() < SB_DEAD_LOG2))
    def _():
        first = top - 2
        stage_logits(0, first)

        def body(state):
            p, _ = state
            b0 = first - 2 * p
            stage_logits(1, b0 - 1)
            stage_values(0, b0)
            stage_logits(0, b0 - 2)
            stage_values(1, b0 - 1)
            return p + 1, decay()

        _, reached = lax.while_loop(lambda state: (state[0] < i - 2) & (state[1] < SB_DEAD_LOG2), body,
                                    (jnp.int32(0), decay()))

        @pl.when(reached < SB_DEAD_LOG2)
        def _():
            stage_logits(1, 0)
            stage_values(0, 1)
            stage_values(1, 0)

    out = jnp.zeros((SB_TQ, LANES), F32)
    for h in range(n_heads):
        acc = acc_ref[h]
        ms = jnp.sum(jnp.where(in_head[h], acc * acc, 0.0), axis=1, keepdims=True) * (1.0 / SB_HEAD_DIM)
        out = out + jnp.where(in_head[h], acc * lax.rsqrt(ms + EPS), 0.0)
    o_ref[0] = (out * gain_ref[...]).astype(o_ref.dtype)


def _sb_attention(qkv, gain, batch, seq):
    pairs = SB_WIDTH // LANES
    n_heads = LANES // SB_HEAD_DIM
    tri = np.tril(np.ones((SB_TK, SB_TK), np.float32), -1)
    return pl.pallas_call(
        _sb_kernel,
        grid=(batch, pairs, seq // SB_TQ),
        in_specs=[pl.BlockSpec((1, SB_TQ, LANES), lambda b, p, i: (b, i, p)),
                  pl.BlockSpec((1, seq, LANES), lambda b, p, i: (b, 0, pairs + p)),
                  pl.BlockSpec((1, seq, LANES), lambda b, p, i: (b, 0, 2 * pairs + p)),
                  pl.BlockSpec((1, LANES), lambda b, p, i: (0, p)),
                  _resident((SB_TK, SB_TK))],
        out_specs=pl.BlockSpec((1, SB_TQ, LANES), lambda b, p, i: (b, i, p)),
        out_shape=jax.ShapeDtypeStruct((batch, seq, SB_WIDTH), BF16),
        scratch_shapes=[pltpu.VMEM((2, n_heads, SB_TQ, SB_TK), BF16),
                        pltpu.VMEM((2, n_heads, SB_TQ, SB_TK), F32),
                        pltpu.VMEM((2, n_heads, SB_TQ, 1), F32),
                        pltpu.VMEM((n_heads, SB_TQ, 1), F32),
                        pltpu.VMEM((n_heads, SB_TQ, LANES), F32)],
        compiler_params=pltpu.CompilerParams(dimension_semantics=("parallel", "parallel", "parallel"),
                                             vmem_limit_bytes=V7X_VMEM_LIMIT),
        name="sb_attn",
    )(qkv, qkv, qkv, gain.reshape(1, SB_WIDTH), jnp.asarray(tri, BF16))


def _hg_levels():
    return [HG_CHUNK >> (l + 1) for l in range(HG_CHUNK.bit_length() - 1)]


def _hg_constants():
    c = HG_CHUNK
    t = np.arange(c)[:, None]
    j = np.arange(c)[None, :]
    sums = [(j <= t), (j > t)]
    masks = [(t == j)]
    for h in _hg_levels():
        pos = t % (2 * h)
        mid = t - pos + h
        upper = pos >= h
        sums.append(np.where(upper, (j >= mid) & (j <= t), (j > t) & (j < mid)))
        masks.append((t // (2 * h) == j // (2 * h)) & upper & (j % (2 * h) < h))
    return (np.concatenate(sums, axis=0).astype(np.float32), np.stack(masks).astype(np.float32))


def _hg_kernel(q_ref, f_ref, i_ref, g_ref, lbl_ref, gain_ref, sums_ref, masks_ref, o_ref, state_ref):
    @pl.when(pl.program_id(2) == 0)
    def _():
        state_ref[...] = jnp.zeros_like(state_ref)

    c = HG_CHUNK
    logits = lbl_ref[...]
    m = jnp.max(logits, axis=0, keepdims=True)
    e = jnp.exp(logits - m)
    lower = e[0:1, :] / jnp.sum(e, axis=0, keepdims=True)
    gain = gain_ref[...]
    n_levels = len(_hg_levels())

    for ci in range(HG_ROWS // c):
        rows = pl.ds(ci * c, c)
        q = _silu(q_ref[0, rows, :])
        forget = lower + (1.0 - lower) * (1.0 / (1.0 + jnp.exp(-f_ref[0, rows, :])))
        k = 1.0 - forget
        v = i_ref[0, rows, :].astype(BF16)
        log_f = jnp.log(forget)
        hi = log_f.astype(BF16)
        lo = (log_f - hi.astype(F32)).astype(BF16)
        sums = _dot(sums_ref[...], jnp.concatenate([hi, lo], axis=1))
        sums = sums[:, :LANES] + sums[:, LANES:]
        b = sums[0:c]
        suffix = sums[c:2 * c]
        scores = masks_ref[0] * _dot_nt(q.astype(BF16), k.astype(BF16))
        for l in range(n_levels):
            decay = jnp.exp(sums[(2 + l) * c:(3 + l) * c])
            scores = scores + masks_ref[1 + l] * _dot_nt((q * decay).astype(BF16), (k * decay).astype(BF16))
        state = state_ref[...]
        o = _dot(scores.astype(BF16), v) + _dot_nt((q * jnp.exp(b)).astype(BF16), state.astype(BF16))
        state_ref[...] = state * jnp.exp(b[c - 1:c, :]) + _dot_tn(v, (k * jnp.exp(suffix)).astype(BF16))
        o = _rmsnorm(o, gain) * _silu(g_ref[0, rows, :])
        o_ref[0, rows, :] = o.astype(o_ref.dtype)


def _hgrn2(hg, lb_logits, gain, batch, seq):
    heads = HG_WIDTH // HG_HEAD_DIM
    sums, masks = _hg_constants()

    def col(k):
        return pl.BlockSpec((1, HG_ROWS, LANES), lambda b, h, t: (b, t, k * heads + h))

    return pl.pallas_call(
        _hg_kernel,
        grid=(batch, heads, seq // HG_ROWS),
        in_specs=[col(0), col(1), col(2), col(3),
                  pl.BlockSpec((2, LANES), lambda b, h, t: (0, h)),
                  pl.BlockSpec((1, LANES), lambda b, h, t: (0, h)),
                  _resident(sums.shape), _resident(masks.shape)],
        out_specs=pl.BlockSpec((1, HG_ROWS, LANES), lambda b, h, t: (b, t, h)),
        out_shape=jax.ShapeDtypeStruct((batch, seq, HG_WIDTH), BF16),
        scratch_shapes=[pltpu.VMEM((HG_HEAD_DIM, HG_HEAD_DIM), F32)],
        compiler_params=pltpu.CompilerParams(dimension_semantics=("parallel", "parallel", "arbitrary"),
                                             vmem_limit_bytes=V7X_VMEM_LIMIT),
        name="hgrn2",
    )(hg, hg, hg, hg, lb_logits, gain.reshape(1, HG_WIDTH), jnp.asarray(sums, BF16), jnp.asarray(masks, F32))


def kernel(x, ffn1_norm, ffn1_w_gate, ffn1_w_up, ffn1_w_down, mix_norm, w_in, sb_out_norm, hg_lower_bound_logits, hg_out_norm, w_out, ffn2_norm, ffn2_w_gate, ffn2_w_up, ffn2_w_down, final_norm):
    batch, seq, d = x.shape
    assert ffn1_norm.shape[0] == 1, "single-layer kernel"
    assert w_in.shape[2] == 3 * SB_WIDTH + 4 * HG_WIDTH and w_out.shape[1] == SB_WIDTH + HG_WIDTH
    assert seq % max(SB_TQ, HG_ROWS) == 0 and (batch * seq) % FFN_ROWS == 0 and SB_TQ == 2 * SB_TK
    n = batch * seq
    bf = lambda w: w[0].astype(BF16)

    x1 = _ffn(x.reshape(n, d), ffn1_norm[0], bf(ffn1_w_gate), bf(ffn1_w_up), bf(ffn1_w_down))
    sb, hg = _proj(x1, mix_norm[0], bf(w_in))
    o_sb = _sb_attention(sb.reshape(batch, seq, -1), sb_out_norm[0], batch, seq)
    o_hg = _hgrn2(hg.reshape(batch, seq, -1), hg_lower_bound_logits, hg_out_norm[0], batch, seq)
    out = _ffn(x1, ffn2_norm[0], bf(ffn2_w_gate), bf(ffn2_w_up), bf(ffn2_w_down),
               mix=(o_sb.reshape(n, -1), o_hg.reshape(n, -1), bf(w_out)), final_gain=final_norm)
    return out.reshape(batch, seq, d)
```

```python
import functools

import jax
import jax.numpy as jnp
import numpy as np
from jax import lax
from jax.experimental import pallas as pl
from jax.experimental.pallas import tpu as pltpu

F32 = jnp.float32
BF16 = jnp.bfloat16

EPS = 1e-6
LANES = 128
SB_HEAD_DIM = 64
SB_WIDTH = 512
HG_HEAD_DIM = 128
HG_WIDTH = 512
V7X_VMEM_LIMIT = 56 * 1024 * 1024

FFN_ROWS = 512
SB_TQ = 512
SB_TK = 256
SB_Q_SCALE = float(np.log2(np.e)) * SB_HEAD_DIM ** -0.5
SB_DEAD_LOG2 = 160.0
HG_CHUNK = 128
HG_ROWS = 512


def _dot(a, b):
    return jnp.dot(a, b, preferred_element_type=F32)


def _dot_nt(a, b):
    return lax.dot_general(a, b, (((1,), (1,)), ((), ())), preferred_element_type=F32)


def _dot_tn(a, b):
    return lax.dot_general(a, b, (((0,), (0,)), ((), ())), preferred_element_type=F32)


def _rmsnorm(x, gain):
    return x * lax.rsqrt(jnp.mean(x * x, axis=-1, keepdims=True) + EPS) * gain


def _silu(x):
    return x * (1.0 / (1.0 + jnp.exp(-x)))


def _resident(shape):
    return pl.BlockSpec(shape, lambda *_: (0,) * len(shape), pipeline_mode=pl.Buffered(1))


def _ffn_kernel(*refs, mix_in, final_norm):
    if mix_in:
        x_ref, osb_ref, ohg_ref, wo_ref, *refs = refs
    else:
        x_ref, *refs = refs
    if final_norm:
        gain_ref, wg_ref, wu_ref, wd_ref, fgain_ref, o_ref = refs
    else:
        gain_ref, wg_ref, wu_ref, wd_ref, o_ref = refs
    x = x_ref[...]
    if mix_in:
        x = x + _dot(osb_ref[...], wo_ref[:SB_WIDTH, :]) + _dot(ohg_ref[...], wo_ref[SB_WIDTH:, :])
    h = _rmsnorm(x, gain_ref[...]).astype(BF16)
    g = _dot(h, wg_ref[...])
    u = _dot(h, wu_ref[...])
    a = (_silu(g) * u).astype(BF16)
    y = x + 0.5 * _dot(a, wd_ref[...])
    if final_norm:
        y = _rmsnorm(y, fgain_ref[...])
    o_ref[...] = y


def _ffn(x, gain, wg, wu, wd, *, mix=None, final_gain=None):
    n, d = x.shape
    f = wg.shape[1]
    rows = pl.BlockSpec((FFN_ROWS, d), lambda i: (i, 0))
    args, specs = [x], [rows]
    if mix is not None:
        osb, ohg, wo = mix
        args += [osb, ohg, wo]
        specs += [pl.BlockSpec((FFN_ROWS, SB_WIDTH), lambda i: (i, 0)),
                  pl.BlockSpec((FFN_ROWS, HG_WIDTH), lambda i: (i, 0)),
                  _resident(wo.shape)]
    args += [gain.reshape(1, d), wg, wu, wd]
    specs += [_resident((1, d)), _resident((d, f)), _resident((d, f)), _resident((f, d))]
    if final_gain is not None:
        args.append(final_gain.reshape(1, d))
        specs.append(_resident((1, d)))
    return pl.pallas_call(
        functools.partial(_ffn_kernel, mix_in=mix is not None, final_norm=final_gain is not None),
        grid=(n // FFN_ROWS,),
        in_specs=specs,
        out_specs=rows,
        out_shape=jax.ShapeDtypeStruct((n, d), F32),
        compiler_params=pltpu.CompilerParams(dimension_semantics=("parallel",),
                                             vmem_limit_bytes=V7X_VMEM_LIMIT),
        name="ffn_mix" if mix is not None else "ffn",
    )(*args)


def _proj_kernel(x_ref, gain_ref, w_ref, sb_ref, hg_ref):
    h = _rmsnorm(x_ref[...], gain_ref[...]).astype(BF16)
    p = _dot(h, w_ref[...])
    sb_cols = sb_ref.shape[1]
    sb_ref[:, :SB_WIDTH] = (p[:, :SB_WIDTH] * SB_Q_SCALE).astype(BF16)
    sb_ref[:, SB_WIDTH:] = p[:, SB_WIDTH:sb_cols].astype(BF16)
    hg_ref[...] = p[:, sb_cols:]


def _proj(x, gain, w_in):
    n, d = x.shape
    sb_cols, hg_cols = 3 * SB_WIDTH, 4 * HG_WIDTH
    return pl.pallas_call(
        _proj_kernel,
        grid=(n // FFN_ROWS,),
        in_specs=[pl.BlockSpec((FFN_ROWS, d), lambda i: (i, 0)), _resident((1, d)), _resident(w_in.shape)],
        out_specs=[pl.BlockSpec((FFN_ROWS, sb_cols), lambda i: (i, 0)),
                   pl.BlockSpec((FFN_ROWS, hg_cols), lambda i: (i, 0))],
        out_shape=[jax.ShapeDtypeStruct((n, sb_cols), BF16), jax.ShapeDtypeStruct((n, hg_cols), F32)],
        compiler_params=pltpu.CompilerParams(dimension_semantics=("parallel",),
                                             vmem_limit_bytes=V7X_VMEM_LIMIT),
        name="proj",
    )(x, gain.reshape(1, d), w_in)


def _sb_kernel(q_ref, k_ref, v_ref, gain_ref, tri_ref, o_ref, sp_buf, lb_buf, rs_buf, carry_ref, acc_ref):
    i = pl.program_id(2)
    half = SB_TQ // 2
    lane = lax.broadcasted_iota(jnp.int32, (1, LANES), 1)
    row = lax.broadcasted_iota(jnp.int32, (SB_TK, SB_TK), 0)
    col = lax.broadcasted_iota(jnp.int32, (SB_TK, SB_TK), 1)
    causal = col < row
    q = q_ref[0]
    n_heads = LANES // SB_HEAD_DIM
    in_head = [(lane >= h * SB_HEAD_DIM) & (lane < (h + 1) * SB_HEAD_DIM) for h in range(n_heads)]
    qh = [jnp.where(m, q, 0.0).astype(BF16) for m in in_head]

    def logits(qrows, kb, mask):
        z = _dot_nt(qrows, kb)
        log1p_term = jnp.log2(1.0 + jnp.exp2(-jnp.abs(z)))
        softplus = jnp.maximum(z + log1p_term, log1p_term)
        log_beta = z - softplus
        if mask is not None:
            softplus = jnp.where(mask, softplus, 0.0)
        return softplus, log_beta

    def weights(softplus_bf16, log_beta, carry):
        tail = _dot(softplus_bf16, tri_ref[...])
        return jnp.exp2(log_beta - tail - carry).astype(BF16)

    def rows_of(kb):
        return pl.ds(pl.multiple_of(kb * SB_TK, SB_TK), SB_TK)

    upper, lower, every = slice(0, half), slice(half, SB_TQ), slice(0, SB_TQ)
    k_lo, v_lo = k_ref[0, rows_of(2 * i), :], v_ref[0, rows_of(2 * i), :]
    k_hi, v_hi = k_ref[0, rows_of(2 * i + 1), :], v_ref[0, rows_of(2 * i + 1), :]
    diag = [(logits(qh[h][upper], k_lo, causal), logits(qh[h][lower], k_hi, causal), logits(qh[h][lower], k_lo, None))
            for h in range(n_heads)]
    v_both = jnp.concatenate([v_hi, v_lo], axis=0)
    for h in range(n_heads):
        (sp_u, lb_u), (sp_l1, lb_l1), (sp_l0, lb_l0) = diag[h]
        carry_l1 = jnp.sum(sp_l1, axis=1, keepdims=True)
        tail = _dot(jnp.concatenate([sp_u, sp_l1, sp_l0], axis=0).astype(BF16), tri_ref[...])
        w_u = jnp.where(causal, jnp.exp2(lb_u - tail[:half]), 0.0).astype(BF16)
        w_l1 = jnp.where(causal, jnp.exp2(lb_l1 - tail[half:2 * half]), 0.0).astype(BF16)
        w_l0 = jnp.exp2(lb_l0 - tail[2 * half:] - carry_l1).astype(BF16)
        carry_ref[h, upper] = jnp.sum(sp_u, axis=1, keepdims=True)
        carry_ref[h, lower] = carry_l1 + jnp.sum(sp_l0, axis=1, keepdims=True)
        acc_ref[h, upper] = _dot(w_u, v_lo)
        acc_ref[h, lower] = _dot(jnp.concatenate([w_l1, w_l0], axis=1), v_both)

    def stage_logits(slot, kb, rows=every):
        k_blk = k_ref[0, rows_of(kb), :]
        for h in range(n_heads):
            softplus, log_beta = logits(qh[h][rows], k_blk, None)
            sp_buf[slot, h, rows] = softplus.astype(BF16)
            lb_buf[slot, h, rows] = log_beta
            rs_buf[slot, h, rows] = jnp.sum(softplus, axis=1, keepdims=True)

    def stage_values(slot, kb, rows=every):
        v_blk = v_ref[0, rows_of(kb), :]
        for h in range(n_heads):
            carry = carry_ref[h, rows]
            w = weights(sp_buf[slot, h, rows], lb_buf[slot, h, rows], carry)
            acc_ref[h, rows] += _dot(w, v_blk)
            carry_ref[h, rows] = carry + rs_buf[slot, h, rows]

    def decay():
        return jnp.min(carry_ref[...])

    top = 2 * i - 1

    @pl.when(i > 0)
    def _():
        lower_dead = jnp.min(carry_ref[:, lower]) >= SB_DEAD_LOG2

        @pl.when(lower_dead)
        def _():
            stage_logits(0, top, upper)
            stage_values(0, top, upper)

        @pl.when(jnp.logical_not(lower_dead))
        def _():
            stage_logits(0, top)
            stage_values(0, top)

        @pl.when(decay() < SB_DEAD_LOG2)
        def _():
            stage_logits(1, top - 1)
            stage_values(1, top - 1)

    @pl.when((i > 1) & (decay() < SB_DEAD_LOG2))
    def _():
        first = top - 2
        stage_logits(0, first)

        def body(state):
            p, _ = state
            b0 = first - 2 * p
            stage_logits(1, b0 - 1)
            stage_values(0, b0)
            stage_logits(0, b0 - 2)
            stage_values(1, b0 - 1)
            return p + 1, decay()

        _, reached = lax.while_loop(lambda state: (state[0] < i - 2) & (state[1] < SB_DEAD_LOG2), body,
                                    (jnp.int32(0), decay()))

        @pl.when(reached < SB_DEAD_LOG2)
        def _():
            stage_logits(1, 0)
            stage_values(0, 1)
            stage_values(1, 0)

    out = jnp.zeros((SB_TQ, LANES), F32)
    for h in range(n_heads):
        acc = acc_ref[h]
        ms = jnp.sum(jnp.where(in_head[h], acc * acc, 0.0), axis=1, keepdims=True) * (1.0 / SB_HEAD_DIM)
        out = out + jnp.where(in_head[h], acc * lax.rsqrt(ms + EPS), 0.0)
    o_ref[0] = (out * gain_ref[...]).astype(o_ref.dtype)


def _sb_attention(qkv, gain, batch, seq):
    pairs = SB_WIDTH // LANES
    n_heads = LANES // SB_HEAD_DIM
    tri = np.tril(np.ones((SB_TK, SB_TK), np.float32), -1)
    return pl.pallas_call(
        _sb_kernel,
        grid=(batch, pairs, seq // SB_TQ),
        in_specs=[pl.BlockSpec((1, SB_TQ, LANES), lambda b, p, i: (b, i, p)),
                  pl.BlockSpec((1, seq, LANES), lambda b, p, i: (b, 0, pairs + p)),
                  pl.BlockSpec((1, seq, LANES), lambda b, p, i: (b, 0, 2 * pairs + p)),
                  pl.BlockSpec((1, LANES), lambda b, p, i: (0, p)),
                  _resident((SB_TK, SB_TK))],
        out_specs=pl.BlockSpec((1, SB_TQ, LANES), lambda b, p, i: (b, i, p)),
        out_shape=jax.ShapeDtypeStruct((batch, seq, SB_WIDTH), BF16),
        scratch_shapes=[pltpu.VMEM((2, n_heads, SB_TQ, SB_TK), BF16),
                        pltpu.VMEM((2, n_heads, SB_TQ, SB_TK), F32),
                        pltpu.VMEM((2, n_heads, SB_TQ, 1), F32),
                        pltpu.VMEM((n_heads, SB_TQ, 1), F32),
                        pltpu.VMEM((n_heads, SB_TQ, LANES), F32)],
        compiler_params=pltpu.CompilerParams(dimension_semantics=("parallel", "parallel", "parallel"),
                                             vmem_limit_bytes=V7X_VMEM_LIMIT),
        name="sb_attn",
    )(qkv, qkv, qkv, gain.reshape(1, SB_WIDTH), jnp.asarray(tri, BF16))


def _hg_levels():
    return [HG_CHUNK >> (l + 1) for l in range(HG_CHUNK.bit_length() - 1)]


def _hg_constants():
    c = HG_CHUNK
    t = np.arange(c)[:, None]
    j = np.arange(c)[None, :]
    masks = [(t == j)]
    for h in _hg_levels():
        masks.append((t // (2 * h) == j // (2 * h)) & (t % (2 * h) >= h) & (j % (2 * h) < h))
    return (j <= t).astype(np.float32), np.stack(masks).astype(np.float32)


def _hg_level_exponent(b, log_f, h):
    c = b.shape[0]
    if h >= 4:
        blocks = b.reshape(c // (2 * h), 2 * h, LANES)
        mid = jnp.broadcast_to(blocks[:, h - 1:h, :], blocks.shape).reshape(c, LANES)
        return -jnp.abs(b - mid)
    pos = lax.broadcasted_iota(jnp.int32, (c, LANES), 0) % (2 * h)
    if h == 1:
        return jnp.where(pos == 1, log_f, 0.0)
    prev = pltpu.roll(log_f, 1, 0)
    nxt = pltpu.roll(log_f, c - 1, 0)
    return jnp.where(pos == 0, nxt, jnp.where(pos == 2, log_f, jnp.where(pos == 3, log_f + prev, 0.0)))


def _hg_kernel(q_ref, f_ref, i_ref, g_ref, lbl_ref, gain_ref, tril_ref, masks_ref, o_ref, state_ref):
    @pl.when(pl.program_id(2) == 0)
    def _():
        state_ref[...] = jnp.zeros_like(state_ref)

    c = HG_CHUNK
    logits = lbl_ref[...]
    m = jnp.max(logits, axis=0, keepdims=True)
    e = jnp.exp(logits - m)
    lower = e[0:1, :] / jnp.sum(e, axis=0, keepdims=True)
    n_chunks = HG_ROWS // c

    def chunk(a, ci):
        return a[ci * c:(ci + 1) * c]

    q = _silu(q_ref[0])
    forget = lower + (1.0 - lower) * (1.0 / (1.0 + jnp.exp(-f_ref[0])))
    k = 1.0 - forget
    v = i_ref[0].astype(BF16)
    log_f = jnp.log(forget)
    hi = log_f.astype(BF16)
    rest = log_f - hi.astype(F32)
    mid = rest.astype(BF16)
    lo = (rest - mid.astype(F32)).astype(BF16)
    terms = jnp.concatenate([hi, mid, lo], axis=1)
    parts = jnp.concatenate([_dot(tril_ref[...], chunk(terms, ci)) for ci in range(n_chunks)], axis=0)
    b = parts[:, :LANES] + parts[:, LANES:2 * LANES] + parts[:, 2 * LANES:]
    b_last = [b[(ci + 1) * c - 1:(ci + 1) * c, :] for ci in range(n_chunks)]

    states = [state_ref[...]]
    for ci in range(n_chunks):
        k_out = (chunk(k, ci) * jnp.exp(b_last[ci] - chunk(b, ci))).astype(BF16)
        states.append(states[ci] * jnp.exp(b_last[ci]) + _dot_tn(chunk(v, ci), k_out))
    state_ref[...] = states[n_chunks]

    q_bf, k_bf = q.astype(BF16), k.astype(BF16)
    scores = [masks_ref[0] * _dot_nt(chunk(q_bf, ci), chunk(k_bf, ci)) for ci in range(n_chunks)]
    for l, h in enumerate(_hg_levels()):
        decay = jnp.exp(_hg_level_exponent(b, log_f, h)).astype(BF16)
        q_l, k_l = q_bf * decay, k_bf * decay
        for ci in range(n_chunks):
            scores[ci] = scores[ci] + masks_ref[1 + l] * _dot_nt(chunk(q_l, ci), chunk(k_l, ci))

    q_in = (q * jnp.exp(b)).astype(BF16)
    outs = [_dot(scores[ci].astype(BF16), chunk(v, ci)) + _dot_nt(chunk(q_in, ci), states[ci].astype(BF16))
            for ci in range(n_chunks)]
    o = _rmsnorm(jnp.concatenate(outs, axis=0), gain_ref[...]) * _silu(g_ref[0])
    o_ref[0] = o.astype(o_ref.dtype)


def _hgrn2(hg, lb_logits, gain, batch, seq):
    heads = HG_WIDTH // HG_HEAD_DIM
    tril, masks = _hg_constants()

    def col(k):
        return pl.BlockSpec((1, HG_ROWS, LANES), lambda b, h, t: (b, t, k * heads + h))

    return pl.pallas_call(
        _hg_kernel,
        grid=(batch, heads, seq // HG_ROWS),
        in_specs=[col(0), col(1), col(2), col(3),
                  pl.BlockSpec((2, LANES), lambda b, h, t: (0, h)),
                  pl.BlockSpec((1, LANES), lambda b, h, t: (0, h)),
                  _resident(tril.shape), _resident(masks.shape)],
        out_specs=pl.BlockSpec((1, HG_ROWS, LANES), lambda b, h, t: (b, t, h)),
        out_shape=jax.ShapeDtypeStruct((batch, seq, HG_WIDTH), BF16),
        scratch_shapes=[pltpu.VMEM((HG_HEAD_DIM, HG_HEAD_DIM), F32)],
        compiler_params=pltpu.CompilerParams(dimension_semantics=("parallel", "parallel", "arbitrary"),
                                             vmem_limit_bytes=V7X_VMEM_LIMIT),
        name="hgrn2",
    )(hg, hg, hg, hg, lb_logits, gain.reshape(1, HG_WIDTH), jnp.asarray(tril, BF16), jnp.asarray(masks, F32))


def kernel(x, ffn1_norm, ffn1_w_gate, ffn1_w_up, ffn1_w_down, mix_norm, w_in, sb_out_norm, hg_lower_bound_logits, hg_out_norm, w_out, ffn2_norm, ffn2_w_gate, ffn2_w_up, ffn2_w_down, final_norm):
    batch, seq, d = x.shape
    assert ffn1_norm.shape[0] == 1, "single-layer kernel"
    assert w_in.shape[2] == 3 * SB_WIDTH + 4 * HG_WIDTH and w_out.shape[1] == SB_WIDTH + HG_WIDTH
    assert seq % max(SB_TQ, HG_ROWS) == 0 and (batch * seq) % FFN_ROWS == 0 and SB_TQ == 2 * SB_TK
    n = batch * seq
    bf = lambda w: w[0].astype(BF16)

    x1 = _ffn(x.reshape(n, d), ffn1_norm[0], bf(ffn1_w_gate), bf(ffn1_w_up), bf(ffn1_w_down))
    sb, hg = _proj(x1, mix_norm[0], bf(w_in))
    o_sb = _sb_attention(sb.reshape(batch, seq, -1), sb_out_norm[0], batch, seq)
    o_hg = _hgrn2(hg.reshape(batch, seq, -1), hg_lower_bound_logits, hg_out_norm[0], batch, seq)
    out = _ffn(x1, ffn2_norm[0], bf(ffn2_w_gate), bf(ffn2_w_up), bf(ffn2_w_down),
               mix=(o_sb.reshape(n, -1), o_hg.reshape(n, -1), bf(w_out)), final_gain=final_norm)
    return out.reshape(batch, seq, d)
```

```python
import functools

import jax
import jax.numpy as jnp
import numpy as np
from jax import lax
from jax.experimental import pallas as pl
from jax.experimental.pallas import tpu as pltpu

F32 = jnp.float32
BF16 = jnp.bfloat16

EPS = 1e-6
LANES = 128
SB_HEAD_DIM = 64
SB_WIDTH = 512
HG_HEAD_DIM = 128
HG_WIDTH = 512
V7X_VMEM_LIMIT = 56 * 1024 * 1024

FFN_ROWS = 512
SB_TQ = 512
SB_TK = 256
SB_Q_SCALE = float(np.log2(np.e)) * SB_HEAD_DIM ** -0.5
SB_DEAD_LOG2 = 160.0
HG_CHUNK = 128
HG_ROWS = 512


def _dot(a, b):
    return jnp.dot(a, b, preferred_element_type=F32)


def _dot_nt(a, b):
    return lax.dot_general(a, b, (((1,), (1,)), ((), ())), preferred_element_type=F32)


def _dot_tn(a, b):
    return lax.dot_general(a, b, (((0,), (0,)), ((), ())), preferred_element_type=F32)


def _rmsnorm(x, gain):
    return x * lax.rsqrt(jnp.mean(x * x, axis=-1, keepdims=True) + EPS) * gain


def _silu(x):
    return x * (1.0 / (1.0 + jnp.exp(-x)))


def _resident(shape):
    return pl.BlockSpec(shape, lambda *_: (0,) * len(shape), pipeline_mode=pl.Buffered(1))


def _ffn_kernel(*refs, mix_in, final_norm):
    if mix_in:
        x_ref, osb_ref, ohg_ref, wo_ref, *refs = refs
    else:
        x_ref, *refs = refs
    if final_norm:
        gain_ref, wg_ref, wu_ref, wd_ref, fgain_ref, o_ref = refs
    else:
        gain_ref, wg_ref, wu_ref, wd_ref, o_ref = refs
    x = x_ref[...]
    if mix_in:
        x = x + _dot(osb_ref[...], wo_ref[:SB_WIDTH, :]) + _dot(ohg_ref[...], wo_ref[SB_WIDTH:, :])
    h = _rmsnorm(x, gain_ref[...]).astype(BF16)
    g = _dot(h, wg_ref[...])
    u = _dot(h, wu_ref[...])
    a = (_silu(g) * u).astype(BF16)
    y = x + 0.5 * _dot(a, wd_ref[...])
    if final_norm:
        y = _rmsnorm(y, fgain_ref[...])
    o_ref[...] = y


def _ffn(x, gain, wg, wu, wd, *, mix=None, final_gain=None):
    n, d = x.shape
    f = wg.shape[1]
    rows = pl.BlockSpec((FFN_ROWS, d), lambda i: (i, 0))
    args, specs = [x], [rows]
    if mix is not None:
        osb, ohg, wo = mix
        args += [osb, ohg, wo]
        specs += [pl.BlockSpec((FFN_ROWS, SB_WIDTH), lambda i: (i, 0)),
                  pl.BlockSpec((FFN_ROWS, HG_WIDTH), lambda i: (i, 0)),
                  _resident(wo.shape)]
    args += [gain.reshape(1, d), wg, wu, wd]
    specs += [_resident((1, d)), _resident((d, f)), _resident((d, f)), _resident((f, d))]
    if final_gain is not None:
        args.append(final_gain.reshape(1, d))
        specs.append(_resident((1, d)))
    return pl.pallas_call(
        functools.partial(_ffn_kernel, mix_in=mix is not None, final_norm=final_gain is not None),
        grid=(n // FFN_ROWS,),
        in_specs=specs,
        out_specs=rows,
        out_shape=jax.ShapeDtypeStruct((n, d), F32),
        compiler_params=pltpu.CompilerParams(dimension_semantics=("parallel",),
                                             vmem_limit_bytes=V7X_VMEM_LIMIT),
        name="ffn_mix" if mix is not None else "ffn",
    )(*args)


def _proj_kernel(x_ref, gain_ref, w_ref, sb_ref, hg_ref):
    h = _rmsnorm(x_ref[...], gain_ref[...]).astype(BF16)
    p = _dot(h, w_ref[...])
    sb_cols = sb_ref.shape[1]
    sb_ref[:, :SB_WIDTH] = (p[:, :SB_WIDTH] * SB_Q_SCALE).astype(BF16)
    sb_ref[:, SB_WIDTH:] = p[:, SB_WIDTH:sb_cols].astype(BF16)
    hg_ref[...] = p[:, sb_cols:]


def _proj(x, gain, w_in):
    n, d = x.shape
    sb_cols, hg_cols = 3 * SB_WIDTH, 4 * HG_WIDTH
    return pl.pallas_call(
        _proj_kernel,
        grid=(n // FFN_ROWS,),
        in_specs=[pl.BlockSpec((FFN_ROWS, d), lambda i: (i, 0)), _resident((1, d)), _resident(w_in.shape)],
        out_specs=[pl.BlockSpec((FFN_ROWS, sb_cols), lambda i: (i, 0)),
                   pl.BlockSpec((FFN_ROWS, hg_cols), lambda i: (i, 0))],
        out_shape=[jax.ShapeDtypeStruct((n, sb_cols), BF16), jax.ShapeDtypeStruct((n, hg_cols), F32)],
        compiler_params=pltpu.CompilerParams(dimension_semantics=("parallel",),
                                             vmem_limit_bytes=V7X_VMEM_LIMIT),
        name="proj",
    )(x, gain.reshape(1, d), w_in)


def _sb_kernel(q_ref, k_ref, v_ref, gain_ref, tri_ref, o_ref, sp_buf, lb_buf, rs_buf, carry_ref, acc_ref):
    i = pl.program_id(2)
    half = SB_TQ // 2
    lane = lax.broadcasted_iota(jnp.int32, (1, LANES), 1)
    row = lax.broadcasted_iota(jnp.int32, (SB_TK, SB_TK), 0)
    col = lax.broadcasted_iota(jnp.int32, (SB_TK, SB_TK), 1)
    causal = col < row
    q = q_ref[0]
    n_heads = LANES // SB_HEAD_DIM
    in_head = [(lane >= h * SB_HEAD_DIM) & (lane < (h + 1) * SB_HEAD_DIM) for h in range(n_heads)]
    qh = [jnp.where(m, q, 0.0).astype(BF16) for m in in_head]

    def logits(qrows, kb, mask):
        z = _dot_nt(qrows, kb)
        log1p_term = jnp.log2(1.0 + jnp.exp2(-jnp.abs(z)))
        softplus = jnp.maximum(z + log1p_term, log1p_term)
        log_beta = z - softplus
        if mask is not None:
            softplus = jnp.where(mask, softplus, 0.0)
        return softplus, log_beta

    def weights(softplus_bf16, log_beta, carry):
        tail = _dot(softplus_bf16, tri_ref[...])
        return jnp.exp2(log_beta - tail - carry).astype(BF16)

    def rows_of(kb):
        return pl.ds(pl.multiple_of(kb * SB_TK, SB_TK), SB_TK)

    upper, lower, every = slice(0, half), slice(half, SB_TQ), slice(0, SB_TQ)
    blocks = [jnp.maximum(2 * i - 1, 0), 2 * i, 2 * i + 1]
    k_near = [k_ref[0, rows_of(kb), :] for kb in blocks]
    v_near = [v_ref[0, rows_of(kb), :] for kb in blocks]
    no_block = jnp.where(i > 0, 0.0, jnp.inf)
    near = [[(logits(qh[h][rows], k_near[r + 1], causal), logits(qh[h][rows], k_near[r], None))
             for r, rows in enumerate((upper, lower))] for h in range(n_heads)]
    for h in range(n_heads):
        tail = _dot(jnp.concatenate([sp for pair in near[h] for sp, _ in pair], axis=0).astype(BF16), tri_ref[...])
        for r, rows in enumerate((upper, lower)):
            (sp_d, lb_d), (sp_f, lb_f) = near[h][r]
            tail_d, tail_f = tail[2 * r * half:(2 * r + 1) * half], tail[(2 * r + 1) * half:(2 * r + 2) * half]
            carry_d = jnp.sum(sp_d, axis=1, keepdims=True)
            w_d = jnp.where(causal, jnp.exp2(lb_d - tail_d), 0.0).astype(BF16)
            w_f = jnp.exp2(lb_f - tail_f - (carry_d + no_block if r == 0 else carry_d)).astype(BF16)
            acc_ref[h, rows] = _dot(jnp.concatenate([w_d, w_f], axis=1),
                                    jnp.concatenate([v_near[r + 1], v_near[r]], axis=0))
            carry_ref[h, rows] = carry_d + jnp.sum(sp_f, axis=1, keepdims=True)

    def stage_logits(slot, kb, rows=every):
        k_blk = k_ref[0, rows_of(kb), :]
        for h in range(n_heads):
            softplus, log_beta = logits(qh[h][rows], k_blk, None)
            sp_buf[slot, h, rows] = softplus.astype(BF16)
            lb_buf[slot, h, rows] = log_beta
            rs_buf[slot, h, rows] = jnp.sum(softplus, axis=1, keepdims=True)

    def stage_values(slot, kb, rows=every):
        v_blk = v_ref[0, rows_of(kb), :]
        for h in range(n_heads):
            carry = carry_ref[h, rows]
            w = weights(sp_buf[slot, h, rows], lb_buf[slot, h, rows], carry)
            acc_ref[h, rows] += _dot(w, v_blk)
            carry_ref[h, rows] = carry + rs_buf[slot, h, rows]

    def decay():
        return jnp.min(carry_ref[...])

    top = 2 * i - 1

    @pl.when(i > 0)
    def _():
        @pl.when(jnp.min(carry_ref[:, lower]) < SB_DEAD_LOG2)
        def _():
            stage_logits(0, top, lower)
            stage_values(0, top, lower)

        @pl.when(decay() < SB_DEAD_LOG2)
        def _():
            stage_logits(1, top - 1)
            stage_values(1, top - 1)

    @pl.when((i > 1) & (decay() < SB_DEAD_LOG2))
    def _():
        first = top - 2
        stage_logits(0, first)

        def body(state):
            p, _ = state
            b0 = first - 2 * p
            stage_logits(1, b0 - 1)
            stage_values(0, b0)
            stage_logits(0, b0 - 2)
            stage_values(1, b0 - 1)
            return p + 1, decay()

        _, reached = lax.while_loop(lambda state: (state[0] < i - 2) & (state[1] < SB_DEAD_LOG2), body,
                                    (jnp.int32(0), decay()))

        @pl.when(reached < SB_DEAD_LOG2)
        def _():
            stage_logits(1, 0)
            stage_values(0, 1)
            stage_values(1, 0)

    out = jnp.zeros((SB_TQ, LANES), F32)
    for h in range(n_heads):
        acc = acc_ref[h]
        ms = jnp.sum(jnp.where(in_head[h], acc * acc, 0.0), axis=1, keepdims=True) * (1.0 / SB_HEAD_DIM)
        out = out + jnp.where(in_head[h], acc * lax.rsqrt(ms + EPS), 0.0)
    o_ref[0] = (out * gain_ref[...]).astype(o_ref.dtype)


def _sb_attention(qkv, gain, batch, seq):
    pairs = SB_WIDTH // LANES
    n_heads = LANES // SB_HEAD_DIM
    tri = np.tril(np.ones((SB_TK, SB_TK), np.float32), -1)
    return pl.pallas_call(
        _sb_kernel,
        grid=(batch, pairs, seq // SB_TQ),
        in_specs=[pl.BlockSpec((1, SB_TQ, LANES), lambda b, p, i: (b, i, p)),
                  pl.BlockSpec((1, seq, LANES), lambda b, p, i: (b, 0, pairs + p)),
                  pl.BlockSpec((1, seq, LANES), lambda b, p, i: (b, 0, 2 * pairs + p)),
                  pl.BlockSpec((1, LANES), lambda b, p, i: (0, p)),
                  _resident((SB_TK, SB_TK))],
        out_specs=pl.BlockSpec((1, SB_TQ, LANES), lambda b, p, i: (b, i, p)),
        out_shape=jax.ShapeDtypeStruct((batch, seq, SB_WIDTH), BF16),
        scratch_shapes=[pltpu.VMEM((2, n_heads, SB_TQ, SB_TK), BF16),
                        pltpu.VMEM((2, n_heads, SB_TQ, SB_TK), F32),
                        pltpu.VMEM((2, n_heads, SB_TQ, 1), F32),
                        pltpu.VMEM((n_heads, SB_TQ, 1), F32),
                        pltpu.VMEM((n_heads, SB_TQ, LANES), F32)],
        compiler_params=pltpu.CompilerParams(dimension_semantics=("parallel", "parallel", "parallel"),
                                             vmem_limit_bytes=V7X_VMEM_LIMIT),
        name="sb_attn",
    )(qkv, qkv, qkv, gain.reshape(1, SB_WIDTH), jnp.asarray(tri, BF16))


def _hg_levels():
    return [HG_CHUNK >> (l + 1) for l in range(HG_CHUNK.bit_length() - 1)]


def _hg_constants():
    c = HG_CHUNK
    t = np.arange(c)[:, None]
    j = np.arange(c)[None, :]
    masks = [(t == j)]
    for h in _hg_levels():
        masks.append((t // (2 * h) == j // (2 * h)) & (t % (2 * h) >= h) & (j % (2 * h) < h))
    return (j <= t).astype(np.float32), np.stack(masks).astype(np.float32)


def _hg_level_exponent(b, log_f, h):
    c = b.shape[0]
    if h >= 4:
        blocks = b.reshape(c // (2 * h), 2 * h, LANES)
        mid = jnp.broadcast_to(blocks[:, h - 1:h, :], blocks.shape).reshape(c, LANES)
        return -jnp.abs(b - mid)
    pos = lax.broadcasted_iota(jnp.int32, (c, LANES), 0) % (2 * h)
    if h == 1:
        return jnp.where(pos == 1, log_f, 0.0)
    prev = pltpu.roll(log_f, 1, 0)
    nxt = pltpu.roll(log_f, c - 1, 0)
    return jnp.where(pos == 0, nxt, jnp.where(pos == 2, log_f, jnp.where(pos == 3, log_f + prev, 0.0)))


def _hg_kernel(q_ref, f_ref, i_ref, g_ref, lbl_ref, gain_ref, tril_ref, masks_ref, o_ref, state_ref):
    @pl.when(pl.program_id(2) == 0)
    def _():
        state_ref[...] = jnp.zeros_like(state_ref)

    c = HG_CHUNK
    logits = lbl_ref[...]
    m = jnp.max(logits, axis=0, keepdims=True)
    e = jnp.exp(logits - m)
    lower = e[0:1, :] / jnp.sum(e, axis=0, keepdims=True)
    n_chunks = HG_ROWS // c

    def chunk(a, ci):
        return a[ci * c:(ci + 1) * c]

    q = _silu(q_ref[0])
    forget = lower + (1.0 - lower) * (1.0 / (1.0 + jnp.exp(-f_ref[0])))
    k = 1.0 - forget
    v = i_ref[0].astype(BF16)
    log_f = jnp.log(forget)
    hi = log_f.astype(BF16)
    rest = log_f - hi.astype(F32)
    mid = rest.astype(BF16)
    lo = (rest - mid.astype(F32)).astype(BF16)
    terms = jnp.concatenate([hi, mid, lo], axis=1)
    parts = jnp.concatenate([_dot(tril_ref[...], chunk(terms, ci)) for ci in range(n_chunks)], axis=0)
    b = parts[:, :LANES] + parts[:, LANES:2 * LANES] + parts[:, 2 * LANES:]
    b_last = [b[(ci + 1) * c - 1:(ci + 1) * c, :] for ci in range(n_chunks)]

    states = [state_ref[...]]
    for ci in range(n_chunks):
        k_out = (chunk(k, ci) * jnp.exp(b_last[ci] - chunk(b, ci))).astype(BF16)
        states.append(states[ci] * jnp.exp(b_last[ci]) + _dot_tn(chunk(v, ci), k_out))
    state_ref[...] = states[n_chunks]

    q_bf, k_bf = q.astype(BF16), k.astype(BF16)
    scores = [masks_ref[0] * _dot_nt(chunk(q_bf, ci), chunk(k_bf, ci)) for ci in range(n_chunks)]
    for l, h in enumerate(_hg_levels()):
        decay = jnp.exp(_hg_level_exponent(b, log_f, h)).astype(BF16)
        q_l, k_l = q_bf * decay, k_bf * decay
        for ci in range(n_chunks):
            scores[ci] = scores[ci] + masks_ref[1 + l] * _dot_nt(chunk(q_l, ci), chunk(k_l, ci))

    q_in = (q * jnp.exp(b)).astype(BF16)
    outs = [_dot(scores[ci].astype(BF16), chunk(v, ci)) + _dot_nt(chunk(q_in, ci), states[ci].astype(BF16))
            for ci in range(n_chunks)]
    o = _rmsnorm(jnp.concatenate(outs, axis=0), gain_ref[...]) * _silu(g_ref[0])
    o_ref[0] = o.astype(o_ref.dtype)


def _hgrn2(hg, lb_logits, gain, batch, seq):
    heads = HG_WIDTH // HG_HEAD_DIM
    tril, masks = _hg_constants()

    def col(k):
        return pl.BlockSpec((1, HG_ROWS, LANES), lambda b, h, t: (b, t, k * heads + h))

    return pl.pallas_call(
        _hg_kernel,
        grid=(batch, heads, seq // HG_ROWS),
        in_specs=[col(0), col(1), col(2), col(3),
                  pl.BlockSpec((2, LANES), lambda b, h, t: (0, h)),
                  pl.BlockSpec((1, LANES), lambda b, h, t: (0, h)),
                  _resident(tril.shape), _resident(masks.shape)],
        out_specs=pl.BlockSpec((1, HG_ROWS, LANES), lambda b, h, t: (b, t, h)),
        out_shape=jax.ShapeDtypeStruct((batch, seq, HG_WIDTH), BF16),
        scratch_shapes=[pltpu.VMEM((HG_HEAD_DIM, HG_HEAD_DIM), F32)],
        compiler_params=pltpu.CompilerParams(dimension_semantics=("parallel", "parallel", "arbitrary"),
                                             vmem_limit_bytes=V7X_VMEM_LIMIT),
        name="hgrn2",
    )(hg, hg, hg, hg, lb_logits, gain.reshape(1, HG_WIDTH), jnp.asarray(tril, BF16), jnp.asarray(masks, F32))


def kernel(x, ffn1_norm, ffn1_w_gate, ffn1_w_up, ffn1_w_down, mix_norm, w_in, sb_out_norm, hg_lower_bound_logits, hg_out_norm, w_out, ffn2_norm, ffn2_w_gate, ffn2_w_up, ffn2_w_down, final_norm):
    batch, seq, d = x.shape
    assert ffn1_norm.shape[0] == 1, "single-layer kernel"
    assert w_in.shape[2] == 3 * SB_WIDTH + 4 * HG_WIDTH and w_out.shape[1] == SB_WIDTH + HG_WIDTH
    assert seq % max(SB_TQ, HG_ROWS) == 0 and (batch * seq) % FFN_ROWS == 0 and SB_TQ == 2 * SB_TK
    n = batch * seq
    bf = lambda w: w[0].astype(BF16)

    x1 = _ffn(x.reshape(n, d), ffn1_norm[0], bf(ffn1_w_gate), bf(ffn1_w_up), bf(ffn1_w_down))
    sb, hg = _proj(x1, mix_norm[0], bf(w_in))
    o_sb = _sb_attention(sb.reshape(batch, seq, -1), sb_out_norm[0], batch, seq)
    o_hg = _hgrn2(hg.reshape(batch, seq, -1), hg_lower_bound_logits, hg_out_norm[0], batch, seq)
    out = _ffn(x1, ffn2_norm[0], bf(ffn2_w_gate), bf(ffn2_w_up), bf(ffn2_w_down),
               mix=(o_sb.reshape(n, -1), o_hg.reshape(n, -1), bf(w_out)), final_gain=final_norm)
    return out.reshape(batch, seq, d)
```

```python
import functools

import jax
import jax.numpy as jnp
import numpy as np
from jax import lax
from jax.experimental import pallas as pl
from jax.experimental.pallas import tpu as pltpu

F32 = jnp.float32
BF16 = jnp.bfloat16

EPS = 1e-6
LANES = 128
SB_HEAD_DIM = 64
SB_WIDTH = 512
HG_HEAD_DIM = 128
HG_WIDTH = 512
V7X_VMEM_LIMIT = 56 * 1024 * 1024

FFN_ROWS = 512
SB_TQ = 512
SB_TK = 256
SB_Q_SCALE = float(np.log2(np.e)) * SB_HEAD_DIM ** -0.5
SB_DEAD_LOG2 = 160.0
HG_CHUNK = 128
HG_ROWS = 1024


def _dot(a, b):
    return jnp.dot(a, b, preferred_element_type=F32)


def _dot_nt(a, b):
    return lax.dot_general(a, b, (((1,), (1,)), ((), ())), preferred_element_type=F32)


def _dot_tn(a, b):
    return lax.dot_general(a, b, (((0,), (0,)), ((), ())), preferred_element_type=F32)


def _rmsnorm(x, gain):
    return x * lax.rsqrt(jnp.mean(x * x, axis=-1, keepdims=True) + EPS) * gain


def _silu(x):
    return x * (1.0 / (1.0 + jnp.exp(-x)))


def _resident(shape):
    return pl.BlockSpec(shape, lambda *_: (0,) * len(shape), pipeline_mode=pl.Buffered(1))


def _ffn_kernel(*refs, mix_in, final_norm, n_cast):
    refs = list(refs)
    x_ref = refs.pop(0)
    if mix_in:
        osb_ref, ohg_ref, wo_ref = refs[:3]
        del refs[:3]
    gain_ref, wg_ref, wu_ref, wd_ref = refs[:4]
    del refs[:4]
    if final_norm:
        fgain_ref = refs.pop(0)
    cast_in, o_ref, cast_out = refs[:n_cast], refs[n_cast], refs[n_cast + 1:]
    x = x_ref[...]
    if mix_in:
        x = x + _dot(osb_ref[...], wo_ref[:SB_WIDTH, :]) + _dot(ohg_ref[...], wo_ref[SB_WIDTH:, :])
    h = _rmsnorm(x, gain_ref[...]).astype(BF16)
    g = _dot(h, wg_ref[...])
    u = _dot(h, wu_ref[...])
    a = (_silu(g) * u).astype(BF16)
    y = x + 0.5 * _dot(a, wd_ref[...])
    if final_norm:
        y = _rmsnorm(y, fgain_ref[...])
    o_ref[...] = y
    for src, dst in zip(cast_in, cast_out):
        dst[...] = src[...].astype(BF16)


def _slab_spec(rows, cols, steps):
    n_slabs = max(k for k in range(1, steps + 1) if steps % k == 0 and rows % (16 * k) == 0)
    return pl.BlockSpec((rows // n_slabs, cols), lambda i: (i // (steps // n_slabs), 0))


def _ffn(x, gain, wg, wu, wd, *, mix=None, final_gain=None, cast=()):
    n, d = x.shape
    f = wg.shape[1]
    steps = n // FFN_ROWS
    rows = pl.BlockSpec((FFN_ROWS, d), lambda i: (i, 0))
    args, specs = [x], [rows]
    if mix is not None:
        osb, ohg, wo = mix
        args += [osb, ohg, wo]
        specs += [pl.BlockSpec((FFN_ROWS, SB_WIDTH), lambda i: (i, 0)),
                  pl.BlockSpec((FFN_ROWS, HG_WIDTH), lambda i: (i, 0)),
                  _resident(wo.shape)]
    args += [gain.reshape(1, d), wg, wu, wd]
    specs += [_resident((1, d)), _resident((d, f)), _resident((d, f)), _resident((f, d))]
    if final_gain is not None:
        args.append(final_gain.reshape(1, d))
        specs.append(_resident((1, d)))
    cast_specs = [_slab_spec(*w.shape, steps) for w in cast]
    out = pl.pallas_call(
        functools.partial(_ffn_kernel, mix_in=mix is not None, final_norm=final_gain is not None, n_cast=len(cast)),
        grid=(steps,),
        in_specs=specs + cast_specs,
        out_specs=[rows] + cast_specs,
        out_shape=[jax.ShapeDtypeStruct((n, d), F32)] + [jax.ShapeDtypeStruct(w.shape, BF16) for w in cast],
        compiler_params=pltpu.CompilerParams(dimension_semantics=("arbitrary",),
                                             vmem_limit_bytes=V7X_VMEM_LIMIT),
        name="ffn_mix" if mix is not None else "ffn",
    )(*args, *cast)
    return out[0] if not cast else out


def _proj_kernel(x_ref, gain_ref, w_ref, sb_ref, hg_ref):
    h = _rmsnorm(x_ref[...], gain_ref[...]).astype(BF16)
    p = _dot(h, w_ref[...])
    sb_cols = sb_ref.shape[1]
    sb_ref[:, :SB_WIDTH] = (p[:, :SB_WIDTH] * SB_Q_SCALE).astype(BF16)
    sb_ref[:, SB_WIDTH:] = p[:, SB_WIDTH:sb_cols].astype(BF16)
    hg_ref[...] = p[:, sb_cols:]


def _proj(x, gain, w_in):
    n, d = x.shape
    sb_cols, hg_cols = 3 * SB_WIDTH, 4 * HG_WIDTH
    return pl.pallas_call(
        _proj_kernel,
        grid=(n // FFN_ROWS,),
        in_specs=[pl.BlockSpec((FFN_ROWS, d), lambda i: (i, 0)), _resident((1, d)), _resident(w_in.shape)],
        out_specs=[pl.BlockSpec((FFN_ROWS, sb_cols), lambda i: (i, 0)),
                   pl.BlockSpec((FFN_ROWS, hg_cols), lambda i: (i, 0))],
        out_shape=[jax.ShapeDtypeStruct((n, sb_cols), BF16), jax.ShapeDtypeStruct((n, hg_cols), F32)],
        compiler_params=pltpu.CompilerParams(dimension_semantics=("parallel",),
                                             vmem_limit_bytes=V7X_VMEM_LIMIT),
        name="proj",
    )(x, gain.reshape(1, d), w_in)


def _sb_kernel(q_ref, k_ref, v_ref, gain_ref, tri_ref, o_ref, sp_buf, lb_buf, rs_buf, carry_ref, acc_ref):
    i = pl.program_id(2)
    half = SB_TQ // 2
    lane = lax.broadcasted_iota(jnp.int32, (1, LANES), 1)
    row = lax.broadcasted_iota(jnp.int32, (SB_TK, SB_TK), 0)
    col = lax.broadcasted_iota(jnp.int32, (SB_TK, SB_TK), 1)
    causal = col < row
    q = q_ref[0]
    n_heads = LANES // SB_HEAD_DIM
    in_head = [(lane >= h * SB_HEAD_DIM) & (lane < (h + 1) * SB_HEAD_DIM) for h in range(n_heads)]
    qh = [jnp.where(m, q, 0.0).astype(BF16) for m in in_head]

    def logits(qrows, kb, mask):
        z = _dot_nt(qrows, kb)
        log1p_term = jnp.log2(1.0 + jnp.exp2(-jnp.abs(z)))
        softplus = jnp.maximum(z + log1p_term, log1p_term)
        log_beta = z - softplus
        if mask is not None:
            softplus = jnp.where(mask, softplus, 0.0)
        return softplus, log_beta

    def weights(softplus_bf16, log_beta, carry):
        tail = _dot(softplus_bf16, tri_ref[...])
        return jnp.exp2(log_beta - tail - carry).astype(BF16)

    def rows_of(kb):
        return pl.ds(pl.multiple_of(kb * SB_TK, SB_TK), SB_TK)

    upper, lower, every = slice(0, half), slice(half, SB_TQ), slice(0, SB_TQ)
    blocks = [jnp.maximum(2 * i - 1, 0), 2 * i, 2 * i + 1]
    k_near = [k_ref[0, rows_of(kb), :] for kb in blocks]
    v_near = [v_ref[0, rows_of(kb), :] for kb in blocks]
    no_block = jnp.where(i > 0, 0.0, jnp.inf)
    near = [[(logits(qh[h][rows], k_near[r + 1], causal), logits(qh[h][rows], k_near[r], None))
             for r, rows in enumerate((upper, lower))] for h in range(n_heads)]
    for h in range(n_heads):
        tail = _dot(jnp.concatenate([sp for pair in near[h] for sp, _ in pair], axis=0).astype(BF16), tri_ref[...])
        for r, rows in enumerate((upper, lower)):
            (sp_d, lb_d), (sp_f, lb_f) = near[h][r]
            tail_d, tail_f = tail[2 * r * half:(2 * r + 1) * half], tail[(2 * r + 1) * half:(2 * r + 2) * half]
            carry_d = jnp.sum(sp_d, axis=1, keepdims=True)
            w_d = jnp.where(causal, jnp.exp2(lb_d - tail_d), 0.0).astype(BF16)
            w_f = jnp.exp2(lb_f - tail_f - (carry_d + no_block if r == 0 else carry_d)).astype(BF16)
            acc_ref[h, rows] = _dot(jnp.concatenate([w_d, w_f], axis=1),
                                    jnp.concatenate([v_near[r + 1], v_near[r]], axis=0))
            carry_ref[h, rows] = carry_d + jnp.sum(sp_f, axis=1, keepdims=True)

    def stage_logits(slot, kb, rows=every):
        k_blk = k_ref[0, rows_of(kb), :]
        for h in range(n_heads):
            softplus, log_beta = logits(qh[h][rows], k_blk, None)
            sp_buf[slot, h, rows] = softplus.astype(BF16)
            lb_buf[slot, h, rows] = log_beta
            rs_buf[slot, h, rows] = jnp.sum(softplus, axis=1, keepdims=True)

    def stage_values(slot, kb, rows=every):
        v_blk = v_ref[0, rows_of(kb), :]
        for h in range(n_heads):
            carry = carry_ref[h, rows]
            w = weights(sp_buf[slot, h, rows], lb_buf[slot, h, rows], carry)
            acc_ref[h, rows] += _dot(w, v_blk)
            carry_ref[h, rows] = carry + rs_buf[slot, h, rows]

    def decay():
        return jnp.min(carry_ref[...])

    top = 2 * i - 1

    @pl.when(i > 0)
    def _():
        @pl.when(jnp.min(carry_ref[:, lower]) < SB_DEAD_LOG2)
        def _():
            stage_logits(0, top, lower)
            stage_values(0, top, lower)

        @pl.when(decay() < SB_DEAD_LOG2)
        def _():
            stage_logits(1, top - 1)
            stage_values(1, top - 1)

    @pl.when((i > 1) & (decay() < SB_DEAD_LOG2))
    def _():
        first = top - 2
        stage_logits(0, first)

        def body(state):
            p, _ = state
            b0 = first - 2 * p
            stage_logits(1, b0 - 1)
            stage_values(0, b0)
            stage_logits(0, b0 - 2)
            stage_values(1, b0 - 1)
            return p + 1, decay()

        _, reached = lax.while_loop(lambda state: (state[0] < i - 2) & (state[1] < SB_DEAD_LOG2), body,
                                    (jnp.int32(0), decay()))

        @pl.when(reached < SB_DEAD_LOG2)
        def _():
            stage_logits(1, 0)
            stage_values(0, 1)
            stage_values(1, 0)

    out = jnp.zeros((SB_TQ, LANES), F32)
    for h in range(n_heads):
        acc = acc_ref[h]
        ms = jnp.sum(jnp.where(in_head[h], acc * acc, 0.0), axis=1, keepdims=True) * (1.0 / SB_HEAD_DIM)
        out = out + jnp.where(in_head[h], acc * lax.rsqrt(ms + EPS), 0.0)
    o_ref[0] = (out * gain_ref[...]).astype(o_ref.dtype)


def _sb_attention(qkv, gain, batch, seq):
    pairs = SB_WIDTH // LANES
    n_heads = LANES // SB_HEAD_DIM
    tri = np.tril(np.ones((SB_TK, SB_TK), np.float32), -1)
    return pl.pallas_call(
        _sb_kernel,
        grid=(batch, pairs, seq // SB_TQ),
        in_specs=[pl.BlockSpec((1, SB_TQ, LANES), lambda b, p, i: (b, i, p)),
                  pl.BlockSpec((1, seq, LANES), lambda b, p, i: (b, 0, pairs + p)),
                  pl.BlockSpec((1, seq, LANES), lambda b, p, i: (b, 0, 2 * pairs + p)),
                  pl.BlockSpec((1, LANES), lambda b, p, i: (0, p)),
                  _resident((SB_TK, SB_TK))],
        out_specs=pl.BlockSpec((1, SB_TQ, LANES), lambda b, p, i: (b, i, p)),
        out_shape=jax.ShapeDtypeStruct((batch, seq, SB_WIDTH), BF16),
        scratch_shapes=[pltpu.VMEM((2, n_heads, SB_TQ, SB_TK), BF16),
                        pltpu.VMEM((2, n_heads, SB_TQ, SB_TK), F32),
                        pltpu.VMEM((2, n_heads, SB_TQ, 1), F32),
                        pltpu.VMEM((n_heads, SB_TQ, 1), F32),
                        pltpu.VMEM((n_heads, SB_TQ, LANES), F32)],
        compiler_params=pltpu.CompilerParams(dimension_semantics=("parallel", "parallel", "parallel"),
                                             vmem_limit_bytes=V7X_VMEM_LIMIT),
        name="sb_attn",
    )(qkv, qkv, qkv, gain.reshape(1, SB_WIDTH), jnp.asarray(tri, BF16))


def _hg_levels():
    return [HG_CHUNK >> (l + 1) for l in range(HG_CHUNK.bit_length() - 1)]


def _hg_constants():
    c = HG_CHUNK
    t = np.arange(c)[:, None]
    j = np.arange(c)[None, :]
    masks = [(t == j)]
    for h in _hg_levels():
        masks.append((t // (2 * h) == j // (2 * h)) & (t % (2 * h) >= h) & (j % (2 * h) < h))
    return (j <= t).astype(np.float32), np.stack(masks).astype(np.float32)


def _hg_level_exponent(b, log_f, h):
    c = b.shape[0]
    if h >= 4:
        blocks = b.reshape(c // (2 * h), 2 * h, LANES)
        mid = jnp.broadcast_to(blocks[:, h - 1:h, :], blocks.shape).reshape(c, LANES)
        return -jnp.abs(b - mid)
    pos = lax.broadcasted_iota(jnp.int32, (c, LANES), 0) % (2 * h)
    if h == 1:
        return jnp.where(pos == 1, log_f, 0.0)
    prev = pltpu.roll(log_f, 1, 0)
    nxt = pltpu.roll(log_f, c - 1, 0)
    return jnp.where(pos == 0, nxt, jnp.where(pos == 2, log_f, jnp.where(pos == 3, log_f + prev, 0.0)))


def _hg_kernel(q_ref, f_ref, i_ref, g_ref, lbl_ref, gain_ref, tril_ref, masks_ref, o_ref, state_ref):
    @pl.when(pl.program_id(2) == 0)
    def _():
        state_ref[...] = jnp.zeros_like(state_ref)

    c = HG_CHUNK
    logits = lbl_ref[...]
    m = jnp.max(logits, axis=0, keepdims=True)
    e = jnp.exp(logits - m)
    lower = e[0:1, :] / jnp.sum(e, axis=0, keepdims=True)
    n_chunks = HG_ROWS // c

    def chunk(a, ci):
        return a[ci * c:(ci + 1) * c]

    q = _silu(q_ref[0])
    forget = lower + (1.0 - lower) * (1.0 / (1.0 + jnp.exp(-f_ref[0])))
    k = 1.0 - forget
    v = i_ref[0].astype(BF16)
    log_f = jnp.log(forget)
    hi = log_f.astype(BF16)
    rest = log_f - hi.astype(F32)
    mid = rest.astype(BF16)
    lo = (rest - mid.astype(F32)).astype(BF16)
    terms = jnp.concatenate([hi, mid, lo], axis=1)
    parts = jnp.concatenate([_dot(tril_ref[...], chunk(terms, ci)) for ci in range(n_chunks)], axis=0)
    b = parts[:, :LANES] + parts[:, LANES:2 * LANES] + parts[:, 2 * LANES:]
    b_last = [b[(ci + 1) * c - 1:(ci + 1) * c, :] for ci in range(n_chunks)]

    states = [state_ref[...]]
    for ci in range(n_chunks):
        k_out = (chunk(k, ci) * jnp.exp(b_last[ci] - chunk(b, ci))).astype(BF16)
        states.append(states[ci] * jnp.exp(b_last[ci]) + _dot_tn(chunk(v, ci), k_out))
    state_ref[...] = states[n_chunks]

    q_bf, k_bf = q.astype(BF16), k.astype(BF16)
    scores = [masks_ref[0] * _dot_nt(chunk(q_bf, ci), chunk(k_bf, ci)) for ci in range(n_chunks)]
    for l, h in enumerate(_hg_levels()):
        decay = jnp.exp(_hg_level_exponent(b, log_f, h)).astype(BF16)
        q_l, k_l = q_bf * decay, k_bf * decay
        for ci in range(n_chunks):
            scores[ci] = scores[ci] + masks_ref[1 + l] * _dot_nt(chunk(q_l, ci), chunk(k_l, ci))

    q_in = (q * jnp.exp(b)).astype(BF16)
    outs = [_dot(scores[ci].astype(BF16), chunk(v, ci)) + _dot_nt(chunk(q_in, ci), states[ci].astype(BF16))
            for ci in range(n_chunks)]
    o = _rmsnorm(jnp.concatenate(outs, axis=0), gain_ref[...]) * _silu(g_ref[0])
    o_ref[0] = o.astype(o_ref.dtype)


def _hgrn2(hg, lb_logits, gain, batch, seq):
    heads = HG_WIDTH // HG_HEAD_DIM
    tril, masks = _hg_constants()

    def col(k):
        return pl.BlockSpec((1, HG_ROWS, LANES), lambda b, h, t: (b, t, k * heads + h))

    return pl.pallas_call(
        _hg_kernel,
        grid=(batch, heads, seq // HG_ROWS),
        in_specs=[col(0), col(1), col(2), col(3),
                  pl.BlockSpec((2, LANES), lambda b, h, t: (0, h)),
                  pl.BlockSpec((1, LANES), lambda b, h, t: (0, h)),
                  _resident(tril.shape), _resident(masks.shape)],
        out_specs=pl.BlockSpec((1, HG_ROWS, LANES), lambda b, h, t: (b, t, h)),
        out_shape=jax.ShapeDtypeStruct((batch, seq, HG_WIDTH), BF16),
        scratch_shapes=[pltpu.VMEM((HG_HEAD_DIM, HG_HEAD_DIM), F32)],
        compiler_params=pltpu.CompilerParams(dimension_semantics=("parallel", "parallel", "arbitrary"),
                                             vmem_limit_bytes=V7X_VMEM_LIMIT),
        name="hgrn2",
    )(hg, hg, hg, hg, lb_logits, gain.reshape(1, HG_WIDTH), jnp.asarray(tril, BF16), jnp.asarray(masks, F32))


def kernel(x, ffn1_norm, ffn1_w_gate, ffn1_w_up, ffn1_w_down, mix_norm, w_in, sb_out_norm, hg_lower_bound_logits, hg_out_norm, w_out, ffn2_norm, ffn2_w_gate, ffn2_w_up, ffn2_w_down, final_norm):
    batch, seq, d = x.shape
    assert ffn1_norm.shape[0] == 1, "single-layer kernel"
    assert w_in.shape[2] == 3 * SB_WIDTH + 4 * HG_WIDTH and w_out.shape[1] == SB_WIDTH + HG_WIDTH
    assert seq % max(SB_TQ, HG_ROWS) == 0 and (batch * seq) % FFN_ROWS == 0 and SB_TQ == 2 * SB_TK
    n = batch * seq
    bf = lambda w: w[0].astype(BF16)

    x1, w_in_bf, w_out_bf, wg2, wu2, wd2 = _ffn(
        x.reshape(n, d), ffn1_norm[0], bf(ffn1_w_gate), bf(ffn1_w_up), bf(ffn1_w_down),
        cast=(w_in[0], w_out[0], ffn2_w_gate[0], ffn2_w_up[0], ffn2_w_down[0]))
    sb, hg = _proj(x1, mix_norm[0], w_in_bf)
    o_sb = _sb_attention(sb.reshape(batch, seq, -1), sb_out_norm[0], batch, seq)
    o_hg = _hgrn2(hg.reshape(batch, seq, -1), hg_lower_bound_logits, hg_out_norm[0], batch, seq)
    out = _ffn(x1, ffn2_norm[0], wg2, wu2, wd2,
               mix=(o_sb.reshape(n, -1), o_hg.reshape(n, -1), w_out_bf), final_gain=final_norm)
    return out.reshape(batch, seq, d)
```

```python
import functools

import jax
import jax.numpy as jnp
import numpy as np
from jax import lax
from jax.experimental import pallas as pl
from jax.experimental.pallas import tpu as pltpu

F32 = jnp.float32
BF16 = jnp.bfloat16

EPS = 1e-6
LANES = 128
SB_HEAD_DIM = 64
SB_WIDTH = 512
HG_HEAD_DIM = 128
HG_WIDTH = 512
V7X_VMEM_LIMIT = 56 * 1024 * 1024

FFN_ROWS = 512
SB_TQ = 512
SB_TK = 256
SB_Q_SCALE = float(np.log2(np.e)) * SB_HEAD_DIM ** -0.5
SB_DEAD_LOG2 = 160.0
HG_CHUNK = 128
HG_ROWS = 1024


def _dot(a, b):
    return jnp.dot(a, b, preferred_element_type=F32)


def _dot_nt(a, b):
    return lax.dot_general(a, b, (((1,), (1,)), ((), ())), preferred_element_type=F32)


def _dot_tn(a, b):
    return lax.dot_general(a, b, (((0,), (0,)), ((), ())), preferred_element_type=F32)


def _rmsnorm(x, gain):
    return x * lax.rsqrt(jnp.mean(x * x, axis=-1, keepdims=True) + EPS) * gain


def _silu(x):
    return x * (1.0 / (1.0 + jnp.exp(-x)))


def _resident(shape):
    return pl.BlockSpec(shape, lambda *_: (0,) * len(shape), pipeline_mode=pl.Buffered(1))


def _ffn_kernel(*refs, mix_in, final_norm, n_cast):
    refs = list(refs)
    x_ref = refs.pop(0)
    if mix_in:
        osb_ref, ohg_ref, wo_ref = refs[:3]
        del refs[:3]
    gain_ref, wg_ref, wu_ref, wd_ref = refs[:4]
    del refs[:4]
    if final_norm:
        fgain_ref = refs.pop(0)
    cast_in, o_ref, cast_out = refs[:n_cast], refs[n_cast], refs[n_cast + 1:]
    x = x_ref[...]
    if mix_in:
        x = x + _dot(osb_ref[...], wo_ref[:SB_WIDTH, :]) + _dot(ohg_ref[...], wo_ref[SB_WIDTH:, :])
    h = _rmsnorm(x, gain_ref[...]).astype(BF16)
    g = _dot(h, wg_ref[...])
    u = _dot(h, wu_ref[...])
    a = (_silu(g) * u).astype(BF16)
    y = x + 0.5 * _dot(a, wd_ref[...])
    if final_norm:
        y = _rmsnorm(y, fgain_ref[...])
    o_ref[...] = y
    for src, dst in zip(cast_in, cast_out):
        dst[...] = src[...].astype(BF16)


def _slab_spec(rows, cols, steps):
    n_slabs = max(k for k in range(1, steps + 1) if steps % k == 0 and rows % (16 * k) == 0)
    return pl.BlockSpec((rows // n_slabs, cols), lambda i: (i // (steps // n_slabs), 0))


def _ffn(x, gain, wg, wu, wd, *, mix=None, final_gain=None, cast=()):
    n, d = x.shape
    f = wg.shape[1]
    steps = n // FFN_ROWS
    rows = pl.BlockSpec((FFN_ROWS, d), lambda i: (i, 0))
    args, specs = [x], [rows]
    if mix is not None:
        osb, ohg, wo = mix
        args += [osb, ohg, wo]
        specs += [pl.BlockSpec((FFN_ROWS, SB_WIDTH), lambda i: (i, 0)),
                  pl.BlockSpec((FFN_ROWS, HG_WIDTH), lambda i: (i, 0)),
                  _resident(wo.shape)]
    args += [gain.reshape(1, d), wg, wu, wd]
    specs += [_resident((1, d)), _resident((d, f)), _resident((d, f)), _resident((f, d))]
    if final_gain is not None:
        args.append(final_gain.reshape(1, d))
        specs.append(_resident((1, d)))
    cast_specs = [_slab_spec(*w.shape, steps) for w in cast]
    out = pl.pallas_call(
        functools.partial(_ffn_kernel, mix_in=mix is not None, final_norm=final_gain is not None, n_cast=len(cast)),
        grid=(steps,),
        in_specs=specs + cast_specs,
        out_specs=[rows] + cast_specs,
        out_shape=[jax.ShapeDtypeStruct((n, d), F32)] + [jax.ShapeDtypeStruct(w.shape, BF16) for w in cast],
        compiler_params=pltpu.CompilerParams(dimension_semantics=("arbitrary",),
                                             vmem_limit_bytes=V7X_VMEM_LIMIT),
        name="ffn_mix" if mix is not None else "ffn",
    )(*args, *cast)
    return out[0] if not cast else out


def _proj_kernel(x_ref, gain_ref, w_ref, sb_ref, hg_ref):
    h = _rmsnorm(x_ref[...], gain_ref[...]).astype(BF16)
    p = _dot(h, w_ref[...])
    sb_cols = sb_ref.shape[1]
    sb_ref[:, :SB_WIDTH] = (p[:, :SB_WIDTH] * SB_Q_SCALE).astype(BF16)
    sb_ref[:, SB_WIDTH:] = p[:, SB_WIDTH:sb_cols].astype(BF16)
    hg_ref[...] = p[:, sb_cols:]


def _proj(x, gain, w_in):
    n, d = x.shape
    sb_cols, hg_cols = 3 * SB_WIDTH, 4 * HG_WIDTH
    return pl.pallas_call(
        _proj_kernel,
        grid=(n // FFN_ROWS,),
        in_specs=[pl.BlockSpec((FFN_ROWS, d), lambda i: (i, 0)), _resident((1, d)), _resident(w_in.shape)],
        out_specs=[pl.BlockSpec((FFN_ROWS, sb_cols), lambda i: (i, 0)),
                   pl.BlockSpec((FFN_ROWS, hg_cols), lambda i: (i, 0))],
        out_shape=[jax.ShapeDtypeStruct((n, sb_cols), BF16), jax.ShapeDtypeStruct((n, hg_cols), F32)],
        compiler_params=pltpu.CompilerParams(dimension_semantics=("parallel",),
                                             vmem_limit_bytes=V7X_VMEM_LIMIT),
        name="proj",
    )(x, gain.reshape(1, d), w_in)


def _sb_kernel(q_ref, k_ref, v_ref, gain_ref, tri_ref, o_ref, sp_buf, lb_buf, rs_buf, carry_ref, acc_ref):
    i = pl.program_id(2)
    half = SB_TQ // 2
    lane = lax.broadcasted_iota(jnp.int32, (1, LANES), 1)
    row = lax.broadcasted_iota(jnp.int32, (SB_TK, SB_TK), 0)
    col = lax.broadcasted_iota(jnp.int32, (SB_TK, SB_TK), 1)
    causal = col < row
    q = q_ref[0]
    n_heads = LANES // SB_HEAD_DIM
    in_head = [(lane >= h * SB_HEAD_DIM) & (lane < (h + 1) * SB_HEAD_DIM) for h in range(n_heads)]
    qh = [jnp.where(m, q, 0.0).astype(BF16) for m in in_head]

    def logits(qrows, kb, mask):
        z = _dot_nt(qrows, kb)
        log1p_term = jnp.log2(1.0 + jnp.exp2(-jnp.abs(z)))
        softplus = jnp.maximum(z + log1p_term, log1p_term)
        log_beta = z - softplus
        if mask is not None:
            softplus = jnp.where(mask, softplus, 0.0)
        return softplus, log_beta

    def weights(softplus_bf16, log_beta, carry):
        tail = _dot(softplus_bf16, tri_ref[...])
        return jnp.exp2(log_beta - tail - carry).astype(BF16)

    def rows_of(kb):
        return pl.ds(pl.multiple_of(kb * SB_TK, SB_TK), SB_TK)

    upper, lower, every = slice(0, half), slice(half, SB_TQ), slice(0, SB_TQ)
    blocks = [jnp.maximum(2 * i - 1, 0), 2 * i, 2 * i + 1]
    k_near = [k_ref[0, rows_of(kb), :] for kb in blocks]
    v_near = [v_ref[0, rows_of(kb), :] for kb in blocks]
    no_block = jnp.where(i > 0, 0.0, jnp.inf)
    near = [[(logits(qh[h][rows], k_near[r + 1], causal), logits(qh[h][rows], k_near[r], None))
             for r, rows in enumerate((upper, lower))] for h in range(n_heads)]
    carries = []
    for h in range(n_heads):
        tail = _dot(jnp.concatenate([sp for pair in near[h] for sp, _ in pair], axis=0).astype(BF16), tri_ref[...])
        for r, rows in enumerate((upper, lower)):
            (sp_d, lb_d), (sp_f, lb_f) = near[h][r]
            tail_d, tail_f = tail[2 * r * half:(2 * r + 1) * half], tail[(2 * r + 1) * half:(2 * r + 2) * half]
            carry_d = jnp.sum(sp_d, axis=1, keepdims=True)
            carries.append(carry_d + jnp.sum(sp_f, axis=1, keepdims=True))
            carry_ref[h, rows] = carries[-1]
            w_d = jnp.where(causal, jnp.exp2(lb_d - tail_d), 0.0).astype(BF16)
            w_f = jnp.exp2(lb_f - tail_f - (carry_d + no_block if r == 0 else carry_d)).astype(BF16)
            acc_ref[h, rows] = _dot(jnp.concatenate([w_d, w_f], axis=1),
                                    jnp.concatenate([v_near[r + 1], v_near[r]], axis=0))
    near_decay = jnp.min(functools.reduce(jnp.minimum, carries))

    def stage_logits(slot, kb, rows=every):
        k_blk = k_ref[0, rows_of(kb), :]
        for h in range(n_heads):
            softplus, log_beta = logits(qh[h][rows], k_blk, None)
            sp_buf[slot, h, rows] = softplus.astype(BF16)
            lb_buf[slot, h, rows] = log_beta
            rs_buf[slot, h, rows] = jnp.sum(softplus, axis=1, keepdims=True)

    def stage_values(slot, kb, rows=every):
        v_blk = v_ref[0, rows_of(kb), :]
        for h in range(n_heads):
            carry = carry_ref[h, rows]
            w = weights(sp_buf[slot, h, rows], lb_buf[slot, h, rows], carry)
            acc_ref[h, rows] += _dot(w, v_blk)
            carry_ref[h, rows] = carry + rs_buf[slot, h, rows]

    def decay():
        return jnp.min(carry_ref[...])

    top = 2 * i - 1

    @pl.when((i > 0) & (near_decay < SB_DEAD_LOG2))
    def _():
        @pl.when(jnp.min(carry_ref[:, lower]) < SB_DEAD_LOG2)
        def _():
            stage_logits(0, top, lower)
            stage_values(0, top, lower)

        @pl.when(decay() < SB_DEAD_LOG2)
        def _():
            stage_logits(1, top - 1)
            stage_values(1, top - 1)

        @pl.when((i > 1) & (decay() < SB_DEAD_LOG2))
        def _():
            first = top - 2
            stage_logits(0, first)

            def body(state):
                p, _ = state
                b0 = first - 2 * p
                stage_logits(1, b0 - 1)
                stage_values(0, b0)
                stage_logits(0, b0 - 2)
                stage_values(1, b0 - 1)
                return p + 1, decay()

            _, reached = lax.while_loop(lambda state: (state[0] < i - 2) & (state[1] < SB_DEAD_LOG2), body,
                                        (jnp.int32(0), decay()))

            @pl.when(reached < SB_DEAD_LOG2)
            def _():
                stage_logits(1, 0)
                stage_values(0, 1)
                stage_values(1, 0)

    out = jnp.zeros((SB_TQ, LANES), F32)
    for h in range(n_heads):
        acc = acc_ref[h]
        ms = jnp.sum(jnp.where(in_head[h], acc * acc, 0.0), axis=1, keepdims=True) * (1.0 / SB_HEAD_DIM)
        out = out + jnp.where(in_head[h], acc * lax.rsqrt(ms + EPS), 0.0)
    o_ref[0] = (out * gain_ref[...]).astype(o_ref.dtype)


def _sb_attention(qkv, gain, batch, seq):
    pairs = SB_WIDTH // LANES
    n_heads = LANES // SB_HEAD_DIM
    tri = np.tril(np.ones((SB_TK, SB_TK), np.float32), -1)
    return pl.pallas_call(
        _sb_kernel,
        grid=(batch, pairs, seq // SB_TQ),
        in_specs=[pl.BlockSpec((1, SB_TQ, LANES), lambda b, p, i: (b, i, p)),
                  pl.BlockSpec((1, seq, LANES), lambda b, p, i: (b, 0, pairs + p)),
                  pl.BlockSpec((1, seq, LANES), lambda b, p, i: (b, 0, 2 * pairs + p)),
                  pl.BlockSpec((1, LANES), lambda b, p, i: (0, p)),
                  _resident((SB_TK, SB_TK))],
        out_specs=pl.BlockSpec((1, SB_TQ, LANES), lambda b, p, i: (b, i, p)),
        out_shape=jax.ShapeDtypeStruct((batch, seq, SB_WIDTH), BF16),
        scratch_shapes=[pltpu.VMEM((2, n_heads, SB_TQ, SB_TK), BF16),
                        pltpu.VMEM((2, n_heads, SB_TQ, SB_TK), F32),
                        pltpu.VMEM((2, n_heads, SB_TQ, 1), F32),
                        pltpu.VMEM((n_heads, SB_TQ, 1), F32),
                        pltpu.VMEM((n_heads, SB_TQ, LANES), F32)],
        compiler_params=pltpu.CompilerParams(dimension_semantics=("parallel", "parallel", "parallel"),
                                             vmem_limit_bytes=V7X_VMEM_LIMIT),
        name="sb_attn",
    )(qkv, qkv, qkv, gain.reshape(1, SB_WIDTH), jnp.asarray(tri, BF16))


def _hg_levels():
    return [HG_CHUNK >> (l + 1) for l in range(HG_CHUNK.bit_length() - 1)]


def _hg_constants():
    c = HG_CHUNK
    t = np.arange(c)[:, None]
    j = np.arange(c)[None, :]
    masks = [(t == j)]
    for h in _hg_levels():
        masks.append((t // (2 * h) == j // (2 * h)) & (t % (2 * h) >= h) & (j % (2 * h) < h))
    return (j <= t).astype(np.float32), np.stack(masks).astype(np.float32)


def _hg_level_exponent(b, log_f, h):
    c = b.shape[0]
    if h >= 4:
        blocks = b.reshape(c // (2 * h), 2 * h, LANES)
        mid = jnp.broadcast_to(blocks[:, h - 1:h, :], blocks.shape).reshape(c, LANES)
        return -jnp.abs(b - mid)
    pos = lax.broadcasted_iota(jnp.int32, (c, LANES), 0) % (2 * h)
    if h == 1:
        return jnp.where(pos == 1, log_f, 0.0)
    prev = pltpu.roll(log_f, 1, 0)
    nxt = pltpu.roll(log_f, c - 1, 0)
    return jnp.where(pos == 0, nxt, jnp.where(pos == 2, log_f, jnp.where(pos == 3, log_f + prev, 0.0)))


def _hg_kernel(q_ref, f_ref, i_ref, g_ref, lbl_ref, gain_ref, tril_ref, masks_ref, o_ref, state_ref):
    @pl.when(pl.program_id(2) == 0)
    def _():
        state_ref[...] = jnp.zeros_like(state_ref)

    c = HG_CHUNK
    logits = lbl_ref[...]
    m = jnp.max(logits, axis=0, keepdims=True)
    e = jnp.exp(logits - m)
    lower = e[0:1, :] / jnp.sum(e, axis=0, keepdims=True)
    n_chunks = HG_ROWS // c

    def chunk(a, ci):
        return a[ci * c:(ci + 1) * c]

    q = _silu(q_ref[0])
    forget = lower + (1.0 - lower) * (1.0 / (1.0 + jnp.exp(-f_ref[0])))
    k = 1.0 - forget
    v = i_ref[0].astype(BF16)
    log_f = jnp.log(forget)
    hi = log_f.astype(BF16)
    rest = log_f - hi.astype(F32)
    mid = rest.astype(BF16)
    lo = (rest - mid.astype(F32)).astype(BF16)
    terms = jnp.concatenate([hi, mid, lo], axis=1)
    parts = jnp.concatenate([_dot(tril_ref[...], chunk(terms, ci)) for ci in range(n_chunks)], axis=0)
    b = parts[:, :LANES] + parts[:, LANES:2 * LANES] + parts[:, 2 * LANES:]
    b_last = [b[(ci + 1) * c - 1:(ci + 1) * c, :] for ci in range(n_chunks)]

    states = [state_ref[...]]
    for ci in range(n_chunks):
        k_out = (chunk(k, ci) * jnp.exp(b_last[ci] - chunk(b, ci))).astype(BF16)
        states.append(states[ci] * jnp.exp(b_last[ci]) + _dot_tn(chunk(v, ci), k_out))
    state_ref[...] = states[n_chunks]

    q_bf, k_bf = q.astype(BF16), k.astype(BF16)
    scores = [masks_ref[0] * _dot_nt(chunk(q_bf, ci), chunk(k_bf, ci)) for ci in range(n_chunks)]
    for l, h in enumerate(_hg_levels()):
        decay = jnp.exp(_hg_level_exponent(b, log_f, h)).astype(BF16)
        q_l, k_l = q_bf * decay, k_bf * decay
        for ci in range(n_chunks):
            scores[ci] = scores[ci] + masks_ref[1 + l] * _dot_nt(chunk(q_l, ci), chunk(k_l, ci))

    q_in = (q * jnp.exp(b)).astype(BF16)
    outs = [_dot(scores[ci].astype(BF16), chunk(v, ci)) + _dot_nt(chunk(q_in, ci), states[ci].astype(BF16))
            for ci in range(n_chunks)]
    o = _rmsnorm(jnp.concatenate(outs, axis=0), gain_ref[...]) * _silu(g_ref[0])
    o_ref[0] = o.astype(o_ref.dtype)


def _hgrn2(hg, lb_logits, gain, batch, seq):
    heads = HG_WIDTH // HG_HEAD_DIM
    tril, masks = _hg_constants()

    def col(k):
        return pl.BlockSpec((1, HG_ROWS, LANES), lambda b, h, t: (b, t, k * heads + h))

    return pl.pallas_call(
        _hg_kernel,
        grid=(batch, heads, seq // HG_ROWS),
        in_specs=[col(0), col(1), col(2), col(3),
                  pl.BlockSpec((2, LANES), lambda b, h, t: (0, h)),
                  pl.BlockSpec((1, LANES), lambda b, h, t: (0, h)),
                  _resident(tril.shape), _resident(masks.shape)],
        out_specs=pl.BlockSpec((1, HG_ROWS, LANES), lambda b, h, t: (b, t, h)),
        out_shape=jax.ShapeDtypeStruct((batch, seq, HG_WIDTH), BF16),
        scratch_shapes=[pltpu.VMEM((HG_HEAD_DIM, HG_HEAD_DIM), F32)],
        compiler_params=pltpu.CompilerParams(dimension_semantics=("parallel", "parallel", "arbitrary"),
                                             vmem_limit_bytes=V7X_VMEM_LIMIT),
        name="hgrn2",
    )(hg, hg, hg, hg, lb_logits, gain.reshape(1, HG_WIDTH), jnp.asarray(tril, BF16), jnp.asarray(masks, F32))


def kernel(x, ffn1_norm, ffn1_w_gate, ffn1_w_up, ffn1_w_down, mix_norm, w_in, sb_out_norm, hg_lower_bound_logits, hg_out_norm, w_out, ffn2_norm, ffn2_w_gate, ffn2_w_up, ffn2_w_down, final_norm):
    batch, seq, d = x.shape
    assert ffn1_norm.shape[0] == 1, "single-layer kernel"
    assert w_in.shape[2] == 3 * SB_WIDTH + 4 * HG_WIDTH and w_out.shape[1] == SB_WIDTH + HG_WIDTH
    assert seq % max(SB_TQ, HG_ROWS) == 0 and (batch * seq) % FFN_ROWS == 0 and SB_TQ == 2 * SB_TK
    n = batch * seq
    bf = lambda w: w[0].astype(BF16)

    x1, w_in_bf, w_out_bf, wg2, wu2, wd2 = _ffn(
        x.reshape(n, d), ffn1_norm[0], bf(ffn1_w_gate), bf(ffn1_w_up), bf(ffn1_w_down),
        cast=(w_in[0], w_out[0], ffn2_w_gate[0], ffn2_w_up[0], ffn2_w_down[0]))
    sb, hg = _proj(x1, mix_norm[0], w_in_bf)
    o_sb = _sb_attention(sb.reshape(batch, seq, -1), sb_out_norm[0], batch, seq)
    o_hg = _hgrn2(hg.reshape(batch, seq, -1), hg_lower_bound_logits, hg_out_norm[0], batch, seq)
    out = _ffn(x1, ffn2_norm[0], wg2, wu2, wd2,
               mix=(o_sb.reshape(n, -1), o_hg.reshape(n, -1), w_out_bf), final_gain=final_norm)
    return out.reshape(batch, seq, d)
```

```python
import functools

import jax
import jax.numpy as jnp
import numpy as np
from jax import lax
from jax.experimental import pallas as pl
from jax.experimental.pallas import tpu as pltpu

F32 = jnp.float32
BF16 = jnp.bfloat16

EPS = 1e-6
LANES = 128
SB_HEAD_DIM = 64
SB_WIDTH = 512
HG_HEAD_DIM = 128
HG_WIDTH = 512
V7X_VMEM_LIMIT = 56 * 1024 * 1024

FFN_ROWS = 512
SB_TQ = 512
SB_TK = 256
SB_Q_SCALE = float(np.log2(np.e)) * SB_HEAD_DIM ** -0.5
SB_DEAD_LOG2 = 160.0
HG_CHUNK = 128
MIX_ROWS = 512


def _dot(a, b):
    return jnp.dot(a, b, preferred_element_type=F32)


def _dot_nt(a, b):
    return lax.dot_general(a, b, (((1,), (1,)), ((), ())), preferred_element_type=F32)


def _dot_tn(a, b):
    return lax.dot_general(a, b, (((0,), (0,)), ((), ())), preferred_element_type=F32)


def _rmsnorm(x, gain):
    return x * lax.rsqrt(jnp.mean(x * x, axis=-1, keepdims=True) + EPS) * gain


def _silu(x):
    return x * (1.0 / (1.0 + jnp.exp(-x)))


def _resident(shape):
    return pl.BlockSpec(shape, lambda *_: (0,) * len(shape), pipeline_mode=pl.Buffered(1))


def _ffn_kernel(*refs, mix_in, final_norm, n_cast):
    refs = list(refs)
    x_ref = refs.pop(0)
    if mix_in:
        osb_ref, ohg_ref, wo_ref = refs[:3]
        del refs[:3]
    gain_ref, wg_ref, wu_ref, wd_ref = refs[:4]
    del refs[:4]
    if final_norm:
        fgain_ref = refs.pop(0)
    cast_in, o_ref, cast_out = refs[:n_cast], refs[n_cast], refs[n_cast + 1:]
    x = x_ref[...]
    if mix_in:
        x = x + _dot(osb_ref[...], wo_ref[:SB_WIDTH, :]) + _dot(ohg_ref[...], wo_ref[SB_WIDTH:, :])
    h = _rmsnorm(x, gain_ref[...]).astype(BF16)
    g = _dot(h, wg_ref[...])
    u = _dot(h, wu_ref[...])
    a = (_silu(g) * u).astype(BF16)
    y = x + 0.5 * _dot(a, wd_ref[...])
    if final_norm:
        y = _rmsnorm(y, fgain_ref[...])
    o_ref[...] = y
    for src, dst in zip(cast_in, cast_out):
        dst[...] = src[...].astype(BF16)


def _slab_spec(rows, cols, steps):
    n_slabs = max(k for k in range(1, steps + 1) if steps % k == 0 and rows % (16 * k) == 0)
    return pl.BlockSpec((rows // n_slabs, cols), lambda i: (i // (steps // n_slabs), 0))


def _ffn(x, gain, wg, wu, wd, *, mix=None, final_gain=None, cast=()):
    n, d = x.shape
    f = wg.shape[1]
    steps = n // FFN_ROWS
    rows = pl.BlockSpec((FFN_ROWS, d), lambda i: (i, 0))
    args, specs = [x], [rows]
    if mix is not None:
        osb, ohg, wo = mix
        args += [osb, ohg, wo]
        specs += [pl.BlockSpec((FFN_ROWS, SB_WIDTH), lambda i: (i, 0)),
                  pl.BlockSpec((FFN_ROWS, HG_WIDTH), lambda i: (i, 0)),
                  _resident(wo.shape)]
    args += [gain.reshape(1, d), wg, wu, wd]
    specs += [_resident((1, d)), _resident((d, f)), _resident((d, f)), _resident((f, d))]
    if final_gain is not None:
        args.append(final_gain.reshape(1, d))
        specs.append(_resident((1, d)))
    cast_specs = [_slab_spec(*w.shape, steps) for w in cast]
    out = pl.pallas_call(
        functools.partial(_ffn_kernel, mix_in=mix is not None, final_norm=final_gain is not None, n_cast=len(cast)),
        grid=(steps,),
        in_specs=specs + cast_specs,
        out_specs=[rows] + cast_specs,
        out_shape=[jax.ShapeDtypeStruct((n, d), F32)] + [jax.ShapeDtypeStruct(w.shape, BF16) for w in cast],
        compiler_params=pltpu.CompilerParams(dimension_semantics=("arbitrary",),
                                             vmem_limit_bytes=V7X_VMEM_LIMIT),
        name="ffn_mix" if mix is not None else "ffn",
    )(*args, *cast)
    return out[0] if not cast else out


def _sb_kernel(q_ref, k_ref, v_ref, gain_ref, tri_ref, o_ref, sp_buf, lb_buf, rs_buf, carry_ref, acc_ref):
    i = pl.program_id(2)
    half = SB_TQ // 2
    lane = lax.broadcasted_iota(jnp.int32, (1, LANES), 1)
    row = lax.broadcasted_iota(jnp.int32, (SB_TK, SB_TK), 0)
    col = lax.broadcasted_iota(jnp.int32, (SB_TK, SB_TK), 1)
    causal = col < row
    q = q_ref[0]
    n_heads = LANES // SB_HEAD_DIM
    in_head = [(lane >= h * SB_HEAD_DIM) & (lane < (h + 1) * SB_HEAD_DIM) for h in range(n_heads)]
    qh = [jnp.where(m, q, 0.0).astype(BF16) for m in in_head]

    def logits(qrows, kb, mask):
        z = _dot_nt(qrows, kb)
        log1p_term = jnp.log2(1.0 + jnp.exp2(-jnp.abs(z)))
        softplus = jnp.maximum(z + log1p_term, log1p_term)
        log_beta = z - softplus
        if mask is not None:
            softplus = jnp.where(mask, softplus, 0.0)
        return softplus, log_beta

    def weights(softplus_bf16, log_beta, carry):
        tail = _dot(softplus_bf16, tri_ref[...])
        return jnp.exp2(log_beta - tail - carry).astype(BF16)

    def rows_of(kb):
        return pl.ds(pl.multiple_of(kb * SB_TK, SB_TK), SB_TK)

    upper, lower, every = slice(0, half), slice(half, SB_TQ), slice(0, SB_TQ)
    blocks = [jnp.maximum(2 * i - 1, 0), 2 * i, 2 * i + 1]
    k_near = [k_ref[0, rows_of(kb), :] for kb in blocks]
    v_near = [v_ref[0, rows_of(kb), :] for kb in blocks]
    no_block = jnp.where(i > 0, 0.0, jnp.inf)
    near = [[(logits(qh[h][rows], k_near[r + 1], causal), logits(qh[h][rows], k_near[r], None))
             for r, rows in enumerate((upper, lower))] for h in range(n_heads)]
    carries = []
    for h in range(n_heads):
        tail = _dot(jnp.concatenate([sp for pair in near[h] for sp, _ in pair], axis=0).astype(BF16), tri_ref[...])
        for r, rows in enumerate((upper, lower)):
            (sp_d, lb_d), (sp_f, lb_f) = near[h][r]
            tail_d, tail_f = tail[2 * r * half:(2 * r + 1) * half], tail[(2 * r + 1) * half:(2 * r + 2) * half]
            carry_d = jnp.sum(sp_d, axis=1, keepdims=True)
            carries.append(carry_d + jnp.sum(sp_f, axis=1, keepdims=True))
            carry_ref[h, rows] = carries[-1]
            w_d = jnp.where(causal, jnp.exp2(lb_d - tail_d), 0.0).astype(BF16)
            w_f = jnp.exp2(lb_f - tail_f - (carry_d + no_block if r == 0 else carry_d)).astype(BF16)
            acc_ref[h, rows] = _dot(jnp.concatenate([w_d, w_f], axis=1),
                                    jnp.concatenate([v_near[r + 1], v_near[r]], axis=0))
    near_decay = jnp.min(functools.reduce(jnp.minimum, carries))

    def stage_logits(slot, kb, rows=every):
        k_blk = k_ref[0, rows_of(kb), :]
        for h in range(n_heads):
            softplus, log_beta = logits(qh[h][rows], k_blk, None)
            sp_buf[slot, h, rows] = softplus.astype(BF16)
            lb_buf[slot, h, rows] = log_beta
            rs_buf[slot, h, rows] = jnp.sum(softplus, axis=1, keepdims=True)

    def stage_values(slot, kb, rows=every):
        v_blk = v_ref[0, rows_of(kb), :]
        for h in range(n_heads):
            carry = carry_ref[h, rows]
            w = weights(sp_buf[slot, h, rows], lb_buf[slot, h, rows], carry)
            acc_ref[h, rows] += _dot(w, v_blk)
            carry_ref[h, rows] = carry + rs_buf[slot, h, rows]

    def decay():
        return jnp.min(carry_ref[...])

    top = 2 * i - 1

    @pl.when((i > 0) & (near_decay < SB_DEAD_LOG2))
    def _():
        @pl.when(jnp.min(carry_ref[:, lower]) < SB_DEAD_LOG2)
        def _():
            stage_logits(0, top, lower)
            stage_values(0, top, lower)

        @pl.when(decay() < SB_DEAD_LOG2)
        def _():
            stage_logits(1, top - 1)
            stage_values(1, top - 1)

        @pl.when((i > 1) & (decay() < SB_DEAD_LOG2))
        def _():
            first = top - 2
            stage_logits(0, first)

            def body(state):
                p, _ = state
                b0 = first - 2 * p
                stage_logits(1, b0 - 1)
                stage_values(0, b0)
                stage_logits(0, b0 - 2)
                stage_values(1, b0 - 1)
                return p + 1, decay()

            _, reached = lax.while_loop(lambda state: (state[0] < i - 2) & (state[1] < SB_DEAD_LOG2), body,
                                        (jnp.int32(0), decay()))

            @pl.when(reached < SB_DEAD_LOG2)
            def _():
                stage_logits(1, 0)
                stage_values(0, 1)
                stage_values(1, 0)

    out = jnp.zeros((SB_TQ, LANES), F32)
    for h in range(n_heads):
        acc = acc_ref[h]
        ms = jnp.sum(jnp.where(in_head[h], acc * acc, 0.0), axis=1, keepdims=True) * (1.0 / SB_HEAD_DIM)
        out = out + jnp.where(in_head[h], acc * lax.rsqrt(ms + EPS), 0.0)
    o_ref[0] = (out * gain_ref[...]).astype(o_ref.dtype)


def _sb_attention(qkv, gain, batch, seq):
    pairs = SB_WIDTH // LANES
    n_heads = LANES // SB_HEAD_DIM
    tri = np.tril(np.ones((SB_TK, SB_TK), np.float32), -1)
    return pl.pallas_call(
        _sb_kernel,
        grid=(batch, pairs, seq // SB_TQ),
        in_specs=[pl.BlockSpec((1, SB_TQ, LANES), lambda b, p, i: (b, i, p)),
                  pl.BlockSpec((1, seq, LANES), lambda b, p, i: (b, 0, pairs + p)),
                  pl.BlockSpec((1, seq, LANES), lambda b, p, i: (b, 0, 2 * pairs + p)),
                  pl.BlockSpec((1, LANES), lambda b, p, i: (0, p)),
                  _resident((SB_TK, SB_TK))],
        out_specs=pl.BlockSpec((1, SB_TQ, LANES), lambda b, p, i: (b, i, p)),
        out_shape=jax.ShapeDtypeStruct((batch, seq, SB_WIDTH), BF16),
        scratch_shapes=[pltpu.VMEM((2, n_heads, SB_TQ, SB_TK), BF16),
                        pltpu.VMEM((2, n_heads, SB_TQ, SB_TK), F32),
                        pltpu.VMEM((2, n_heads, SB_TQ, 1), F32),
                        pltpu.VMEM((n_heads, SB_TQ, 1), F32),
                        pltpu.VMEM((n_heads, SB_TQ, LANES), F32)],
        compiler_params=pltpu.CompilerParams(dimension_semantics=("parallel", "parallel", "parallel"),
                                             vmem_limit_bytes=V7X_VMEM_LIMIT),
        name="sb_attn",
    )(qkv, qkv, qkv, gain.reshape(1, SB_WIDTH), jnp.asarray(tri, BF16))


def _hg_levels():
    return [HG_CHUNK >> (l + 1) for l in range(HG_CHUNK.bit_length() - 1)]


def _hg_constants():
    c = HG_CHUNK
    t = np.arange(c)[:, None]
    j = np.arange(c)[None, :]
    masks = [(t == j)]
    for h in _hg_levels():
        masks.append((t // (2 * h) == j // (2 * h)) & (t % (2 * h) >= h) & (j % (2 * h) < h))
    return (j <= t).astype(np.float32), np.stack(masks).astype(np.float32)


def _hg_level_exponent(b, log_f, h):
    c = b.shape[0]
    if h >= 4:
        blocks = b.reshape(c // (2 * h), 2 * h, LANES)
        mid = jnp.broadcast_to(blocks[:, h - 1:h, :], blocks.shape).reshape(c, LANES)
        return -jnp.abs(b - mid)
    pos = lax.broadcasted_iota(jnp.int32, (c, LANES), 0) % (2 * h)
    if h == 1:
        return jnp.where(pos == 1, log_f, 0.0)
    prev = pltpu.roll(log_f, 1, 0)
    nxt = pltpu.roll(log_f, c - 1, 0)
    return jnp.where(pos == 0, nxt, jnp.where(pos == 2, log_f, jnp.where(pos == 3, log_f + prev, 0.0)))


def _hg_head(q_raw, f_raw, v_raw, g_raw, lower, gain, state, tril_ref, masks_ref, side_jobs):
    c = HG_CHUNK
    n_chunks = q_raw.shape[0] // c
    side_jobs = iter(side_jobs)

    def side_job():
        next(side_jobs, lambda: None)()

    def chunk(a, ci):
        return a[ci * c:(ci + 1) * c]

    q = _silu(q_raw)
    forget = lower + (1.0 - lower) * (1.0 / (1.0 + jnp.exp(-f_raw)))
    k = 1.0 - forget
    v = v_raw.astype(BF16)
    log_f = jnp.log(forget)
    hi = log_f.astype(BF16)
    rest = log_f - hi.astype(F32)
    mid = rest.astype(BF16)
    lo = (rest - mid.astype(F32)).astype(BF16)
    terms = jnp.concatenate([hi, mid, lo], axis=1)
    parts = jnp.concatenate([_dot(tril_ref[...], chunk(terms, ci)) for ci in range(n_chunks)], axis=0)
    b = parts[:, :LANES] + parts[:, LANES:2 * LANES] + parts[:, 2 * LANES:]
    b_last = [b[(ci + 1) * c - 1:(ci + 1) * c, :] for ci in range(n_chunks)]

    states = [state]
    for ci in range(n_chunks):
        k_out = (chunk(k, ci) * jnp.exp(b_last[ci] - chunk(b, ci))).astype(BF16)
        states.append(states[ci] * jnp.exp(b_last[ci]) + _dot_tn(chunk(v, ci), k_out))
    side_job()

    q_bf, k_bf = q.astype(BF16), k.astype(BF16)
    scores = [masks_ref[0] * _dot_nt(chunk(q_bf, ci), chunk(k_bf, ci)) for ci in range(n_chunks)]
    for l, h in enumerate(_hg_levels()):
        decay = jnp.exp(_hg_level_exponent(b, log_f, h)).astype(BF16)
        q_l, k_l = q_bf * decay, k_bf * decay
        for ci in range(n_chunks):
            scores[ci] = scores[ci] + masks_ref[1 + l] * _dot_nt(chunk(q_l, ci), chunk(k_l, ci))
        side_job()

    q_in = (q * jnp.exp(b)).astype(BF16)
    outs = [_dot(scores[ci].astype(BF16), chunk(v, ci)) + _dot_nt(chunk(q_in, ci), states[ci].astype(BF16))
            for ci in range(n_chunks)]
    o = _rmsnorm(jnp.concatenate(outs, axis=0), gain) * _silu(g_raw)
    for _ in side_jobs:
        _()
    return o.astype(BF16), states[n_chunks]


def _mix_in_kernel(x_ref, ngain_ref, w_ref, lbl_ref, hgain_ref, tril_ref, masks_ref, sb_ref, o_ref,
                   h_scr, hg_scr, state_ref):
    @pl.when(pl.program_id(1) == 0)
    def _():
        state_ref[...] = jnp.zeros_like(state_ref)

    heads = HG_WIDTH // HG_HEAD_DIM
    sb_cols = sb_ref.shape[1]
    h_scr[...] = _rmsnorm(x_ref[...], ngain_ref[...]).astype(BF16)

    def project_head(h, part):
        def job():
            cols = [sb_cols + (2 * part + j) * HG_WIDTH + h * HG_HEAD_DIM for j in range(2)]
            w = jnp.concatenate([w_ref[:, c0:c0 + HG_HEAD_DIM] for c0 in cols], axis=1)
            hg_scr[h, :, 2 * part * HG_HEAD_DIM:(2 * part + 2) * HG_HEAD_DIM] = _dot(h_scr[...], w)
        return job

    def project_sb(block, width=2 * LANES):
        def job():
            cols = slice(block * width, (block + 1) * width)
            p = _dot(h_scr[...], w_ref[:, cols])
            sb_ref[:, cols] = ((p * SB_Q_SCALE) if block * width < SB_WIDTH else p).astype(BF16)
        return job

    logits = lbl_ref[...]
    e = jnp.exp(logits - jnp.max(logits, axis=0, keepdims=True))
    lower = e[0:1, :] / jnp.sum(e, axis=0, keepdims=True)

    project_head(0, 0)()
    project_head(0, 1)()
    for h in range(heads):
        lanes = slice(h * HG_HEAD_DIM, (h + 1) * HG_HEAD_DIM)
        if h + 1 < heads:
            jobs = [project_head(h + 1, 0), lambda: None, lambda: None, project_head(h + 1, 1)]
        else:
            jobs = [project_sb(blk) for blk in range(sb_cols // (2 * LANES))]
        q, f, v, g = (hg_scr[h, :, j * HG_HEAD_DIM:(j + 1) * HG_HEAD_DIM] for j in range(4))
        o, state_ref[h] = _hg_head(q, f, v, g, lower[:, lanes], hgain_ref[:, lanes], state_ref[h],
                                   tril_ref, masks_ref, jobs)
        o_ref[:, lanes] = o


def _mix_in(x, gain, w_in, lb_logits, hg_gain, batch, seq):
    n, d = x.shape
    sb_cols = 3 * SB_WIDTH
    heads = HG_WIDTH // HG_HEAD_DIM
    tiles = seq // MIX_ROWS
    tril, masks = _hg_constants()
    rows = lambda width: pl.BlockSpec((MIX_ROWS, width), lambda b, t: (b * tiles + t, 0))
    return pl.pallas_call(
        _mix_in_kernel,
        grid=(batch, tiles),
        in_specs=[rows(d), _resident((1, d)), _resident(w_in.shape), _resident(lb_logits.shape),
                  _resident((1, HG_WIDTH)), _resident(tril.shape), _resident(masks.shape)],
        out_specs=[rows(sb_cols), rows(HG_WIDTH)],
        out_shape=[jax.ShapeDtypeStruct((n, sb_cols), BF16), jax.ShapeDtypeStruct((n, HG_WIDTH), BF16)],
        scratch_shapes=[pltpu.VMEM((MIX_ROWS, d), BF16),
                        pltpu.VMEM((heads, MIX_ROWS, 4 * HG_HEAD_DIM), F32),
                        pltpu.VMEM((heads, HG_HEAD_DIM, HG_HEAD_DIM), F32)],
        compiler_params=pltpu.CompilerParams(dimension_semantics=("parallel", "arbitrary"),
                                             vmem_limit_bytes=V7X_VMEM_LIMIT),
        name="mix_in",
    )(x, gain.reshape(1, d), w_in, lb_logits, hg_gain.reshape(1, HG_WIDTH),
      jnp.asarray(tril, BF16), jnp.asarray(masks, F32))


def kernel(x, ffn1_norm, ffn1_w_gate, ffn1_w_up, ffn1_w_down, mix_norm, w_in, sb_out_norm, hg_lower_bound_logits, hg_out_norm, w_out, ffn2_norm, ffn2_w_gate, ffn2_w_up, ffn2_w_down, final_norm):
    batch, seq, d = x.shape
    assert ffn1_norm.shape[0] == 1, "single-layer kernel"
    assert w_in.shape[2] == 3 * SB_WIDTH + 4 * HG_WIDTH and w_out.shape[1] == SB_WIDTH + HG_WIDTH
    assert seq % max(SB_TQ, MIX_ROWS) == 0 and (batch * seq) % FFN_ROWS == 0 and SB_TQ == 2 * SB_TK
    n = batch * seq
    bf = lambda w: w[0].astype(BF16)

    x1, w_in_bf, w_out_bf, wg2, wu2, wd2 = _ffn(
        x.reshape(n, d), ffn1_norm[0], bf(ffn1_w_gate), bf(ffn1_w_up), bf(ffn1_w_down),
        cast=(w_in[0], w_out[0], ffn2_w_gate[0], ffn2_w_up[0], ffn2_w_down[0]))
    sb, o_hg = _mix_in(x1, mix_norm[0], w_in_bf, hg_lower_bound_logits, hg_out_norm[0], batch, seq)
    o_sb = _sb_attention(sb.reshape(batch, seq, -1), sb_out_norm[0], batch, seq)
    out = _ffn(x1, ffn2_norm[0], wg2, wu2, wd2, mix=(o_sb.reshape(n, -1), o_hg, w_out_bf), final_gain=final_norm)
    return out.reshape(batch, seq, d)
```

```python
import functools

import jax
import jax.numpy as jnp
import numpy as np
from jax import lax
from jax.experimental import pallas as pl
from jax.experimental.pallas import tpu as pltpu

F32 = jnp.float32
BF16 = jnp.bfloat16

EPS = 1e-6
LANES = 128
SB_HEAD_DIM = 64
SB_WIDTH = 512
HG_HEAD_DIM = 128
HG_WIDTH = 512
V7X_VMEM_LIMIT = 56 * 1024 * 1024

FFN_ROWS = 512
SB_TQ = 512
SB_TK = 256
SB_Q_SCALE = float(np.log2(np.e)) * SB_HEAD_DIM ** -0.5
SB_DEAD_LOG2 = 160.0
HG_CHUNK = 128
MIX_ROWS = 512


def _dot(a, b):
    return jnp.dot(a, b, preferred_element_type=F32)


def _dot_nt(a, b):
    return lax.dot_general(a, b, (((1,), (1,)), ((), ())), preferred_element_type=F32)


def _dot_tn(a, b):
    return lax.dot_general(a, b, (((0,), (0,)), ((), ())), preferred_element_type=F32)


def _rmsnorm(x, gain):
    return x * lax.rsqrt(jnp.mean(x * x, axis=-1, keepdims=True) + EPS) * gain


def _silu(x):
    return x * (1.0 / (1.0 + jnp.exp(-x)))


def _resident(shape):
    return pl.BlockSpec(shape, lambda *_: (0,) * len(shape), pipeline_mode=pl.Buffered(1))


def _ffn_kernel(*refs, mix_in, final_norm, n_cast):
    refs = list(refs)
    x_ref = refs.pop(0)
    if mix_in:
        osb_ref, ohg_ref, wo_ref = refs[:3]
        del refs[:3]
    gain_ref, wg_ref, wu_ref, wd_ref = refs[:4]
    del refs[:4]
    if final_norm:
        fgain_ref = refs.pop(0)
    cast_in, o_ref, cast_out = refs[:n_cast], refs[n_cast], refs[n_cast + 1:]
    x = x_ref[...]
    if mix_in:
        x = x + _dot(osb_ref[...], wo_ref[:SB_WIDTH, :]) + _dot(ohg_ref[...], wo_ref[SB_WIDTH:, :])
    h = _rmsnorm(x, gain_ref[...]).astype(BF16)
    g = _dot(h, wg_ref[...])
    u = _dot(h, wu_ref[...])
    a = (_silu(g) * u).astype(BF16)
    y = x + 0.5 * _dot(a, wd_ref[...])
    if final_norm:
        y = _rmsnorm(y, fgain_ref[...])
    o_ref[...] = y
    for src, dst in zip(cast_in, cast_out):
        dst[...] = src[...].astype(BF16)


def _slab_spec(rows, cols, steps):
    n_slabs = max(k for k in range(1, steps + 1) if steps % k == 0 and rows % (16 * k) == 0)
    return pl.BlockSpec((rows // n_slabs, cols), lambda i: (i // (steps // n_slabs), 0))


def _ffn(x, gain, wg, wu, wd, *, mix=None, final_gain=None, cast=()):
    n, d = x.shape
    f = wg.shape[1]
    steps = n // FFN_ROWS
    rows = pl.BlockSpec((FFN_ROWS, d), lambda i: (i, 0))
    args, specs = [x], [rows]
    if mix is not None:
        osb, ohg, wo = mix
        args += [osb, ohg, wo]
        specs += [pl.BlockSpec((FFN_ROWS, SB_WIDTH), lambda i: (i, 0)),
                  pl.BlockSpec((FFN_ROWS, HG_WIDTH), lambda i: (i, 0)),
                  _resident(wo.shape)]
    args += [gain.reshape(1, d), wg, wu, wd]
    specs += [_resident((1, d)), _resident((d, f)), _resident((d, f)), _resident((f, d))]
    if final_gain is not None:
        args.append(final_gain.reshape(1, d))
        specs.append(_resident((1, d)))
    cast_specs = [_slab_spec(*w.shape, steps) for w in cast]
    out = pl.pallas_call(
        functools.partial(_ffn_kernel, mix_in=mix is not None, final_norm=final_gain is not None, n_cast=len(cast)),
        grid=(steps,),
        in_specs=specs + cast_specs,
        out_specs=[rows] + cast_specs,
        out_shape=[jax.ShapeDtypeStruct((n, d), F32)] + [jax.ShapeDtypeStruct(w.shape, BF16) for w in cast],
        compiler_params=pltpu.CompilerParams(dimension_semantics=("arbitrary",),
                                             vmem_limit_bytes=V7X_VMEM_LIMIT),
        name="ffn_mix" if mix is not None else "ffn",
    )(*args, *cast)
    return out[0] if not cast else out


def _sb_kernel(q_ref, k_ref, v_ref, gain_ref, tri_ref, o_ref, sp_buf, lb_buf, rs_buf, carry_ref, acc_ref):
    i = pl.program_id(2)
    half = SB_TQ // 2
    lane = lax.broadcasted_iota(jnp.int32, (1, LANES), 1)
    row = lax.broadcasted_iota(jnp.int32, (SB_TK, SB_TK), 0)
    col = lax.broadcasted_iota(jnp.int32, (SB_TK, SB_TK), 1)
    causal = col < row
    q = q_ref[0]
    n_heads = LANES // SB_HEAD_DIM
    in_head = [(lane >= h * SB_HEAD_DIM) & (lane < (h + 1) * SB_HEAD_DIM) for h in range(n_heads)]
    qh = [jnp.where(m, q, 0.0).astype(BF16) for m in in_head]

    def logits(qrows, kb, mask):
        z = _dot_nt(qrows, kb)
        log1p_term = jnp.log2(1.0 + jnp.exp2(-jnp.abs(z)))
        softplus = jnp.maximum(z + log1p_term, log1p_term)
        log_beta = z - softplus
        if mask is not None:
            softplus = jnp.where(mask, softplus, 0.0)
        return softplus, log_beta

    def weights(softplus_bf16, log_beta, carry):
        tail = _dot(softplus_bf16, tri_ref[...])
        return jnp.exp2(log_beta - tail - carry).astype(BF16)

    def rows_of(kb):
        return pl.ds(pl.multiple_of(kb * SB_TK, SB_TK), SB_TK)

    upper, lower, every = slice(0, half), slice(half, SB_TQ), slice(0, SB_TQ)
    blocks = [jnp.maximum(2 * i - 1, 0), 2 * i, 2 * i + 1]
    k_near = [k_ref[0, rows_of(kb), :] for kb in blocks]
    v_near = [v_ref[0, rows_of(kb), :] for kb in blocks]
    no_block = jnp.where(i > 0, 0.0, jnp.inf)
    near = [[(logits(qh[h][rows], k_near[r + 1], causal), logits(qh[h][rows], k_near[r], None))
             for r, rows in enumerate((upper, lower))] for h in range(n_heads)]
    carries = []
    for h in range(n_heads):
        tail = _dot(jnp.concatenate([sp for pair in near[h] for sp, _ in pair], axis=0).astype(BF16), tri_ref[...])
        for r, rows in enumerate((upper, lower)):
            (sp_d, lb_d), (sp_f, lb_f) = near[h][r]
            tail_d, tail_f = tail[2 * r * half:(2 * r + 1) * half], tail[(2 * r + 1) * half:(2 * r + 2) * half]
            carry_d = jnp.sum(sp_d, axis=1, keepdims=True)
            carries.append(carry_d + jnp.sum(sp_f, axis=1, keepdims=True))
            carry_ref[h, rows] = carries[-1]
            w_d = jnp.where(causal, jnp.exp2(lb_d - tail_d), 0.0).astype(BF16)
            w_f = jnp.exp2(lb_f - tail_f - (carry_d + no_block if r == 0 else carry_d)).astype(BF16)
            acc_ref[h, rows] = _dot(jnp.concatenate([w_d, w_f], axis=1),
                                    jnp.concatenate([v_near[r + 1], v_near[r]], axis=0))
    near_decay = jnp.min(functools.reduce(jnp.minimum, carries))

    def stage_logits(slot, kb, rows=every):
        k_blk = k_ref[0, rows_of(kb), :]
        for h in range(n_heads):
            softplus, log_beta = logits(qh[h][rows], k_blk, None)
            sp_buf[slot, h, rows] = softplus.astype(BF16)
            lb_buf[slot, h, rows] = log_beta
            rs_buf[slot, h, rows] = jnp.sum(softplus, axis=1, keepdims=True)

    def stage_values(slot, kb, rows=every):
        v_blk = v_ref[0, rows_of(kb), :]
        for h in range(n_heads):
            carry = carry_ref[h, rows]
            w = weights(sp_buf[slot, h, rows], lb_buf[slot, h, rows], carry)
            acc_ref[h, rows] += _dot(w, v_blk)
            carry_ref[h, rows] = carry + rs_buf[slot, h, rows]

    def decay():
        return jnp.min(carry_ref[...])

    top = 2 * i - 1

    def write_output():
        out = jnp.zeros((SB_TQ, LANES), F32)
        for h in range(n_heads):
            acc = acc_ref[h]
            ms = jnp.sum(jnp.where(in_head[h], acc * acc, 0.0), axis=1, keepdims=True) * (1.0 / SB_HEAD_DIM)
            out = out + jnp.where(in_head[h], acc * lax.rsqrt(ms + EPS), 0.0)
        o_ref[0] = (out * gain_ref[...]).astype(o_ref.dtype)

    write_output()

    @pl.when((i > 0) & (near_decay < SB_DEAD_LOG2))
    def _():
        @pl.when(jnp.min(carry_ref[:, lower]) < SB_DEAD_LOG2)
        def _():
            stage_logits(0, top, lower)
            stage_values(0, top, lower)

        @pl.when(decay() < SB_DEAD_LOG2)
        def _():
            stage_logits(1, top - 1)
            stage_values(1, top - 1)

        @pl.when((i > 1) & (decay() < SB_DEAD_LOG2))
        def _():
            first = top - 2
            stage_logits(0, first)

            def body(state):
                p, _ = state
                b0 = first - 2 * p
                stage_logits(1, b0 - 1)
                stage_values(0, b0)
                stage_logits(0, b0 - 2)
                stage_values(1, b0 - 1)
                return p + 1, decay()

            _, reached = lax.while_loop(lambda state: (state[0] < i - 2) & (state[1] < SB_DEAD_LOG2), body,
                                        (jnp.int32(0), decay()))

            @pl.when(reached < SB_DEAD_LOG2)
            def _():
                stage_logits(1, 0)
                stage_values(0, 1)
                stage_values(1, 0)

        write_output()


def _sb_attention(qkv, gain, batch, seq):
    pairs = SB_WIDTH // LANES
    n_heads = LANES // SB_HEAD_DIM
    tri = np.tril(np.ones((SB_TK, SB_TK), np.float32), -1)
    return pl.pallas_call(
        _sb_kernel,
        grid=(batch, pairs, seq // SB_TQ),
        in_specs=[pl.BlockSpec((1, SB_TQ, LANES), lambda b, p, i: (b, i, p)),
                  pl.BlockSpec((1, seq, LANES), lambda b, p, i: (b, 0, pairs + p)),
                  pl.BlockSpec((1, seq, LANES), lambda b, p, i: (b, 0, 2 * pairs + p)),
                  pl.BlockSpec((1, LANES), lambda b, p, i: (0, p)),
                  _resident((SB_TK, SB_TK))],
        out_specs=pl.BlockSpec((1, SB_TQ, LANES), lambda b, p, i: (b, i, p)),
        out_shape=jax.ShapeDtypeStruct((batch, seq, SB_WIDTH), BF16),
        scratch_shapes=[pltpu.VMEM((2, n_heads, SB_TQ, SB_TK), BF16),
                        pltpu.VMEM((2, n_heads, SB_TQ, SB_TK), F32),
                        pltpu.VMEM((2, n_heads, SB_TQ, 1), F32),
                        pltpu.VMEM((n_heads, SB_TQ, 1), F32),
                        pltpu.VMEM((n_heads, SB_TQ, LANES), F32)],
        compiler_params=pltpu.CompilerParams(dimension_semantics=("parallel", "parallel", "parallel"),
                                             vmem_limit_bytes=V7X_VMEM_LIMIT),
        name="sb_attn",
    )(qkv, qkv, qkv, gain.reshape(1, SB_WIDTH), jnp.asarray(tri, BF16))


def _hg_levels():
    return [HG_CHUNK >> (l + 1) for l in range(HG_CHUNK.bit_length() - 1)]


def _hg_constants():
    c = HG_CHUNK
    t = np.arange(c)[:, None]
    j = np.arange(c)[None, :]
    masks = [(t == j)]
    for h in _hg_levels():
        masks.append((t // (2 * h) == j // (2 * h)) & (t % (2 * h) >= h) & (j % (2 * h) < h))
    return (j <= t).astype(np.float32), np.stack(masks).astype(np.float32)


def _hg_level_exponent(b, log_f, h):
    c = b.shape[0]
    if h >= 4:
        blocks = b.reshape(c // (2 * h), 2 * h, LANES)
        mid = jnp.broadcast_to(blocks[:, h - 1:h, :], blocks.shape).reshape(c, LANES)
        return -jnp.abs(b - mid)
    pos = lax.broadcasted_iota(jnp.int32, (c, LANES), 0) % (2 * h)
    if h == 1:
        return jnp.where(pos == 1, log_f, 0.0)
    prev = pltpu.roll(log_f, 1, 0)
    nxt = pltpu.roll(log_f, c - 1, 0)
    return jnp.where(pos == 0, nxt, jnp.where(pos == 2, log_f, jnp.where(pos == 3, log_f + prev, 0.0)))


def _hg_head(q_raw, f_raw, v_raw, g_raw, lower, gain, state, tril_ref, masks_ref, side_jobs):
    c = HG_CHUNK
    n_chunks = q_raw.shape[0] // c
    side_jobs = iter(side_jobs)

    def side_job():
        next(side_jobs, lambda: None)()

    def chunk(a, ci):
        return a[ci * c:(ci + 1) * c]

    q = _silu(q_raw)
    forget = lower + (1.0 - lower) * (1.0 / (1.0 + jnp.exp(-f_raw)))
    k = 1.0 - forget
    v = v_raw.astype(BF16)
    log_f = jnp.log(forget)
    hi = log_f.astype(BF16)
    rest = log_f - hi.astype(F32)
    mid = rest.astype(BF16)
    lo = (rest - mid.astype(F32)).astype(BF16)
    terms = jnp.concatenate([hi, mid, lo], axis=1)
    parts = jnp.concatenate([_dot(tril_ref[...], chunk(terms, ci)) for ci in range(n_chunks)], axis=0)
    b = parts[:, :LANES] + parts[:, LANES:2 * LANES] + parts[:, 2 * LANES:]
    b_last = [b[(ci + 1) * c - 1:(ci + 1) * c, :] for ci in range(n_chunks)]

    states = [state]
    for ci in range(n_chunks):
        k_out = (chunk(k, ci) * jnp.exp(b_last[ci] - chunk(b, ci))).astype(BF16)
        states.append(states[ci] * jnp.exp(b_last[ci]) + _dot_tn(chunk(v, ci), k_out))
    side_job()

    q_bf, k_bf = q.astype(BF16), k.astype(BF16)
    scores = [masks_ref[0] * _dot_nt(chunk(q_bf, ci), chunk(k_bf, ci)) for ci in range(n_chunks)]
    for l, h in enumerate(_hg_levels()):
        decay = jnp.exp(_hg_level_exponent(b, log_f, h)).astype(BF16)
        q_l, k_l = q_bf * decay, k_bf * decay
        for ci in range(n_chunks):
            scores[ci] = scores[ci] + masks_ref[1 + l] * _dot_nt(chunk(q_l, ci), chunk(k_l, ci))
        side_job()

    q_in = (q * jnp.exp(b)).astype(BF16)
    outs = [_dot(scores[ci].astype(BF16), chunk(v, ci)) + _dot_nt(chunk(q_in, ci), states[ci].astype(BF16))
            for ci in range(n_chunks)]
    o = _rmsnorm(jnp.concatenate(outs, axis=0), gain) * _silu(g_raw)
    for _ in side_jobs:
        _()
    return o.astype(BF16), states[n_chunks]


def _mix_in_kernel(x_ref, ngain_ref, w_ref, lbl_ref, hgain_ref, tril_ref, masks_ref, sb_ref, o_ref,
                   h_scr, hg_scr, state_ref):
    @pl.when(pl.program_id(1) == 0)
    def _():
        state_ref[...] = jnp.zeros_like(state_ref)

    heads = HG_WIDTH // HG_HEAD_DIM
    sb_cols = sb_ref.shape[1]
    h_scr[...] = _rmsnorm(x_ref[...], ngain_ref[...]).astype(BF16)

    def project_head(h, part):
        def job():
            cols = [sb_cols + (2 * part + j) * HG_WIDTH + h * HG_HEAD_DIM for j in range(2)]
            w = jnp.concatenate([w_ref[:, c0:c0 + HG_HEAD_DIM] for c0 in cols], axis=1)
            hg_scr[h, :, 2 * part * HG_HEAD_DIM:(2 * part + 2) * HG_HEAD_DIM] = _dot(h_scr[...], w)
        return job

    def project_sb(block, width=2 * LANES):
        def job():
            cols = slice(block * width, (block + 1) * width)
            p = _dot(h_scr[...], w_ref[:, cols])
            sb_ref[:, cols] = ((p * SB_Q_SCALE) if block * width < SB_WIDTH else p).astype(BF16)
        return job

    logits = lbl_ref[...]
    e = jnp.exp(logits - jnp.max(logits, axis=0, keepdims=True))
    lower = e[0:1, :] / jnp.sum(e, axis=0, keepdims=True)

    project_head(0, 0)()
    project_head(0, 1)()
    for h in range(heads):
        lanes = slice(h * HG_HEAD_DIM, (h + 1) * HG_HEAD_DIM)
        if h + 1 < heads:
            jobs = [lambda: None, project_head(h + 1, 0), lambda: None, lambda: None, lambda: None, project_head(h + 1, 1)]
        else:
            jobs = [project_sb(blk) for blk in range(sb_cols // (2 * LANES))]
        q, f, v, g = (hg_scr[h, :, j * HG_HEAD_DIM:(j + 1) * HG_HEAD_DIM] for j in range(4))
        o, state_ref[h] = _hg_head(q, f, v, g, lower[:, lanes], hgain_ref[:, lanes], state_ref[h],
                                   tril_ref, masks_ref, jobs)
        o_ref[:, lanes] = o


def _mix_in(x, gain, w_in, lb_logits, hg_gain, batch, seq):
    n, d = x.shape
    sb_cols = 3 * SB_WIDTH
    heads = HG_WIDTH // HG_HEAD_DIM
    tiles = seq // MIX_ROWS
    tril, masks = _hg_constants()
    rows = lambda width: pl.BlockSpec((MIX_ROWS, width), lambda b, t: (b * tiles + t, 0))
    return pl.pallas_call(
        _mix_in_kernel,
        grid=(batch, tiles),
        in_specs=[rows(d), _resident((1, d)), _resident(w_in.shape), _resident(lb_logits.shape),
                  _resident((1, HG_WIDTH)), _resident(tril.shape), _resident(masks.shape)],
        out_specs=[rows(sb_cols), rows(HG_WIDTH)],
        out_shape=[jax.ShapeDtypeStruct((n, sb_cols), BF16), jax.ShapeDtypeStruct((n, HG_WIDTH), BF16)],
        scratch_shapes=[pltpu.VMEM((MIX_ROWS, d), BF16),
                        pltpu.VMEM((heads, MIX_ROWS, 4 * HG_HEAD_DIM), F32),
                        pltpu.VMEM((heads, HG_HEAD_DIM, HG_HEAD_DIM), F32)],
        compiler_params=pltpu.CompilerParams(dimension_semantics=("parallel", "arbitrary"),
                                             vmem_limit_bytes=V7X_VMEM_LIMIT),
        name="mix_in",
    )(x, gain.reshape(1, d), w_in, lb_logits, hg_gain.reshape(1, HG_WIDTH),
      jnp.asarray(tril, BF16), jnp.asarray(masks, F32))


def kernel(x, ffn1_norm, ffn1_w_gate, ffn1_w_up, ffn1_w_down, mix_norm, w_in, sb_out_norm, hg_lower_bound_logits, hg_out_norm, w_out, ffn2_norm, ffn2_w_gate, ffn2_w_up, ffn2_w_down, final_norm):
    batch, seq, d = x.shape
    assert ffn1_norm.shape[0] == 1, "single-layer kernel"
    assert w_in.shape[2] == 3 * SB_WIDTH + 4 * HG_WIDTH and w_out.shape[1] == SB_WIDTH + HG_WIDTH
    assert seq % max(SB_TQ, MIX_ROWS) == 0 and (batch * seq) % FFN_ROWS == 0 and SB_TQ == 2 * SB_TK
    n = batch * seq
    bf = lambda w: w[0].astype(BF16)

    x1, w_in_bf, w_out_bf, wg2, wu2, wd2 = _ffn(
        x.reshape(n, d), ffn1_norm[0], bf(ffn1_w_gate), bf(ffn1_w_up), bf(ffn1_w_down),
        cast=(w_in[0], w_out[0], ffn2_w_gate[0], ffn2_w_up[0], ffn2_w_down[0]))
    sb, o_hg = _mix_in(x1, mix_norm[0], w_in_bf, hg_lower_bound_logits, hg_out_norm[0], batch, seq)
    o_sb = _sb_attention(sb.reshape(batch, seq, -1), sb_out_norm[0], batch, seq)
    out = _ffn(x1, ffn2_norm[0], wg2, wu2, wd2, mix=(o_sb.reshape(n, -1), o_hg, w_out_bf), final_gain=final_norm)
    return out.reshape(batch, seq, d)
```

```python
import functools

import jax
import jax.numpy as jnp
import numpy as np
from jax import lax
from jax.experimental import pallas as pl
from jax.experimental.pallas import tpu as pltpu

F32 = jnp.float32
BF16 = jnp.bfloat16

EPS = 1e-6
LANES = 128
SB_HEAD_DIM = 64
SB_WIDTH = 512
HG_HEAD_DIM = 128
HG_WIDTH = 512
V7X_VMEM_LIMIT = 56 * 1024 * 1024

FFN_ROWS = 512
SB_TQ = 512
SB_TK = 256
SB_Q_SCALE = float(np.log2(np.e)) * SB_HEAD_DIM ** -0.5
SB_DEAD_LOG2 = 1e30
HG_CHUNK = 128
MIX_ROWS = 512


def _dot(a, b):
    return jnp.dot(a, b, preferred_element_type=F32)


def _dot_nt(a, b):
    return lax.dot_general(a, b, (((1,), (1,)), ((), ())), preferred_element_type=F32)


def _dot_tn(a, b):
    return lax.dot_general(a, b, (((0,), (0,)), ((), ())), preferred_element_type=F32)


def _rmsnorm(x, gain):
    return x * lax.rsqrt(jnp.mean(x * x, axis=-1, keepdims=True) + EPS) * gain


def _silu(x):
    return x * (1.0 / (1.0 + jnp.exp(-x)))


def _resident(shape):
    return pl.BlockSpec(shape, lambda *_: (0,) * len(shape), pipeline_mode=pl.Buffered(1))


def _ffn_kernel(*refs, mix_in, final_norm, n_cast):
    refs = list(refs)
    x_ref = refs.pop(0)
    if mix_in:
        osb_ref, ohg_ref, wo_ref = refs[:3]
        del refs[:3]
    gain_ref, wg_ref, wu_ref, wd_ref = refs[:4]
    del refs[:4]
    if final_norm:
        fgain_ref = refs.pop(0)
    cast_in, o_ref, cast_out = refs[:n_cast], refs[n_cast], refs[n_cast + 1:]
    x = x_ref[...]
    if mix_in:
        x = x + _dot(osb_ref[...], wo_ref[:SB_WIDTH, :]) + _dot(ohg_ref[...], wo_ref[SB_WIDTH:, :])
    h = _rmsnorm(x, gain_ref[...]).astype(BF16)
    g = _dot(h, wg_ref[...])
    u = _dot(h, wu_ref[...])
    a = (_silu(g) * u).astype(BF16)
    y = x + 0.5 * _dot(a, wd_ref[...])
    if final_norm:
        y = _rmsnorm(y, fgain_ref[...])
    o_ref[...] = y
    for src, dst in zip(cast_in, cast_out):
        dst[...] = src[...].astype(BF16)


def _slab_spec(rows, cols, steps):
    n_slabs = max(k for k in range(1, steps + 1) if steps % k == 0 and rows % (16 * k) == 0)
    return pl.BlockSpec((rows // n_slabs, cols), lambda i: (i // (steps // n_slabs), 0))


def _ffn(x, gain, wg, wu, wd, *, mix=None, final_gain=None, cast=()):
    n, d = x.shape
    f = wg.shape[1]
    steps = n // FFN_ROWS
    rows = pl.BlockSpec((FFN_ROWS, d), lambda i: (i, 0))
    args, specs = [x], [rows]
    if mix is not None:
        osb, ohg, wo = mix
        args += [osb, ohg, wo]
        specs += [pl.BlockSpec((FFN_ROWS, SB_WIDTH), lambda i: (i, 0)),
                  pl.BlockSpec((FFN_ROWS, HG_WIDTH), lambda i: (i, 0)),
                  _resident(wo.shape)]
    args += [gain.reshape(1, d), wg, wu, wd]
    specs += [_resident((1, d)), _resident((d, f)), _resident((d, f)), _resident((f, d))]
    if final_gain is not None:
        args.append(final_gain.reshape(1, d))
        specs.append(_resident((1, d)))
    cast_specs = [_slab_spec(*w.shape, steps) for w in cast]
    out = pl.pallas_call(
        functools.partial(_ffn_kernel, mix_in=mix is not None, final_norm=final_gain is not None, n_cast=len(cast)),
        grid=(steps,),
        in_specs=specs + cast_specs,
        out_specs=[rows] + cast_specs,
        out_shape=[jax.ShapeDtypeStruct((n, d), F32)] + [jax.ShapeDtypeStruct(w.shape, BF16) for w in cast],
        compiler_params=pltpu.CompilerParams(dimension_semantics=("arbitrary",),
                                             vmem_limit_bytes=V7X_VMEM_LIMIT),
        name="ffn_mix" if mix is not None else "ffn",
    )(*args, *cast)
    return out[0] if not cast else out


def _sb_kernel(q_ref, k_ref, v_ref, gain_ref, tri_ref, o_ref, sp_buf, lb_buf, rs_buf, carry_ref, acc_ref):
    i = pl.program_id(2)
    half = SB_TQ // 2
    lane = lax.broadcasted_iota(jnp.int32, (1, LANES), 1)
    row = lax.broadcasted_iota(jnp.int32, (SB_TK, SB_TK), 0)
    col = lax.broadcasted_iota(jnp.int32, (SB_TK, SB_TK), 1)
    causal = col < row
    q = q_ref[0]
    n_heads = LANES // SB_HEAD_DIM
    in_head = [(lane >= h * SB_HEAD_DIM) & (lane < (h + 1) * SB_HEAD_DIM) for h in range(n_heads)]
    qh = [jnp.where(m, q, 0.0).astype(BF16) for m in in_head]

    def logits(qrows, kb, mask):
        z = _dot_nt(qrows, kb)
        log1p_term = jnp.log2(1.0 + jnp.exp2(-jnp.abs(z)))
        softplus = jnp.maximum(z + log1p_term, log1p_term)
        log_beta = z - softplus
        if mask is not None:
            softplus = jnp.where(mask, softplus, 0.0)
        return softplus, log_beta

    def weights(softplus_bf16, log_beta, carry):
        tail = _dot(softplus_bf16, tri_ref[...])
        return jnp.exp2(log_beta - tail - carry).astype(BF16)

    def rows_of(kb):
        return pl.ds(pl.multiple_of(kb * SB_TK, SB_TK), SB_TK)

    upper, lower, every = slice(0, half), slice(half, SB_TQ), slice(0, SB_TQ)
    blocks = [jnp.maximum(2 * i - 1, 0), 2 * i, 2 * i + 1]
    k_near = [k_ref[0, rows_of(kb), :] for kb in blocks]
    v_near = [v_ref[0, rows_of(kb), :] for kb in blocks]
    no_block = jnp.where(i > 0, 0.0, jnp.inf)
    near = [[(logits(qh[h][rows], k_near[r + 1], causal), logits(qh[h][rows], k_near[r], None))
             for r, rows in enumerate((upper, lower))] for h in range(n_heads)]
    carries = []
    for h in range(n_heads):
        tail = _dot(jnp.concatenate([sp for pair in near[h] for sp, _ in pair], axis=0).astype(BF16), tri_ref[...])
        for r, rows in enumerate((upper, lower)):
            (sp_d, lb_d), (sp_f, lb_f) = near[h][r]
            tail_d, tail_f = tail[2 * r * half:(2 * r + 1) * half], tail[(2 * r + 1) * half:(2 * r + 2) * half]
            carry_d = jnp.sum(sp_d, axis=1, keepdims=True)
            carries.append(carry_d + jnp.sum(sp_f, axis=1, keepdims=True))
            carry_ref[h, rows] = carries[-1]
            w_d = jnp.where(causal, jnp.exp2(lb_d - tail_d), 0.0).astype(BF16)
            w_f = jnp.exp2(lb_f - tail_f - (carry_d + no_block if r == 0 else carry_d)).astype(BF16)
            acc_ref[h, rows] = _dot(jnp.concatenate([w_d, w_f], axis=1),
                                    jnp.concatenate([v_near[r + 1], v_near[r]], axis=0))
    near_decay = jnp.min(functools.reduce(jnp.minimum, carries))

    def stage_logits(slot, kb, rows=every):
        k_blk = k_ref[0, rows_of(kb), :]
        for h in range(n_heads):
            softplus, log_beta = logits(qh[h][rows], k_blk, None)
            sp_buf[slot, h, rows] = softplus.astype(BF16)
            lb_buf[slot, h, rows] = log_beta
            rs_buf[slot, h, rows] = jnp.sum(softplus, axis=1, keepdims=True)

    def stage_values(slot, kb, rows=every):
        v_blk = v_ref[0, rows_of(kb), :]
        for h in range(n_heads):
            carry = carry_ref[h, rows]
            w = weights(sp_buf[slot, h, rows], lb_buf[slot, h, rows], carry)
            acc_ref[h, rows] += _dot(w, v_blk)
            carry_ref[h, rows] = carry + rs_buf[slot, h, rows]

    def decay():
        return jnp.min(carry_ref[...])

    top = 2 * i - 1

    def write_output():
        out = jnp.zeros((SB_TQ, LANES), F32)
        for h in range(n_heads):
            acc = acc_ref[h]
            ms = jnp.sum(jnp.where(in_head[h], acc * acc, 0.0), axis=1, keepdims=True) * (1.0 / SB_HEAD_DIM)
            out = out + jnp.where(in_head[h], acc * lax.rsqrt(ms + EPS), 0.0)
        o_ref[0] = (out * gain_ref[...]).astype(o_ref.dtype)

    write_output()

    @pl.when((i > 0) & (near_decay < SB_DEAD_LOG2))
    def _():
        @pl.when(jnp.min(carry_ref[:, lower]) < SB_DEAD_LOG2)
        def _():
            stage_logits(0, top, lower)
            stage_values(0, top, lower)

        @pl.when(decay() < SB_DEAD_LOG2)
        def _():
            stage_logits(1, top - 1)
            stage_values(1, top - 1)

        @pl.when((i > 1) & (decay() < SB_DEAD_LOG2))
        def _():
            first = top - 2
            stage_logits(0, first)

            def body(state):
                p, _ = state
                b0 = first - 2 * p
                stage_logits(1, b0 - 1)
                stage_values(0, b0)
                stage_logits(0, b0 - 2)
                stage_values(1, b0 - 1)
                return p + 1, decay()

            _, reached = lax.while_loop(lambda state: (state[0] < i - 2) & (state[1] < SB_DEAD_LOG2), body,
                                        (jnp.int32(0), decay()))

            @pl.when(reached < SB_DEAD_LOG2)
            def _():
                stage_logits(1, 0)
                stage_values(0, 1)
                stage_values(1, 0)

        write_output()


def _sb_attention(qkv, gain, batch, seq):
    pairs = SB_WIDTH // LANES
    n_heads = LANES // SB_HEAD_DIM
    tri = np.tril(np.ones((SB_TK, SB_TK), np.float32), -1)
    return pl.pallas_call(
        _sb_kernel,
        grid=(batch, pairs, seq // SB_TQ),
        in_specs=[pl.BlockSpec((1, SB_TQ, LANES), lambda b, p, i: (b, i, p)),
                  pl.BlockSpec((1, seq, LANES), lambda b, p, i: (b, 0, pairs + p)),
                  pl.BlockSpec((1, seq, LANES), lambda b, p, i: (b, 0, 2 * pairs + p)),
                  pl.BlockSpec((1, LANES), lambda b, p, i: (0, p)),
                  _resident((SB_TK, SB_TK))],
        out_specs=pl.BlockSpec((1, SB_TQ, LANES), lambda b, p, i: (b, i, p)),
        out_shape=jax.ShapeDtypeStruct((batch, seq, SB_WIDTH), BF16),
        scratch_shapes=[pltpu.VMEM((2, n_heads, SB_TQ, SB_TK), BF16),
                        pltpu.VMEM((2, n_heads, SB_TQ, SB_TK), F32),
                        pltpu.VMEM((2, n_heads, SB_TQ, 1), F32),
                        pltpu.VMEM((n_heads, SB_TQ, 1), F32),
                        pltpu.VMEM((n_heads, SB_TQ, LANES), F32)],
        compiler_params=pltpu.CompilerParams(dimension_semantics=("parallel", "parallel", "parallel"),
                                             vmem_limit_bytes=V7X_VMEM_LIMIT),
        name="sb_attn",
    )(qkv, qkv, qkv, gain.reshape(1, SB_WIDTH), jnp.asarray(tri, BF16))


def _hg_levels():
    return [HG_CHUNK >> (l + 1) for l in range(HG_CHUNK.bit_length() - 1)]


def _hg_constants():
    c = HG_CHUNK
    t = np.arange(c)[:, None]
    j = np.arange(c)[None, :]
    masks = [(t == j)]
    for h in _hg_levels():
        masks.append((t // (2 * h) == j // (2 * h)) & (t % (2 * h) >= h) & (j % (2 * h) < h))
    return (j <= t).astype(np.float32), np.stack(masks).astype(np.float32)


def _hg_level_exponent(b, log_f, h):
    c = b.shape[0]
    if h >= 4:
        blocks = b.reshape(c // (2 * h), 2 * h, LANES)
        mid = jnp.broadcast_to(blocks[:, h - 1:h, :], blocks.shape).reshape(c, LANES)
        return -jnp.abs(b - mid)
    pos = lax.broadcasted_iota(jnp.int32, (c, LANES), 0) % (2 * h)
    if h == 1:
        return jnp.where(pos == 1, log_f, 0.0)
    prev = pltpu.roll(log_f, 1, 0)
    nxt = pltpu.roll(log_f, c - 1, 0)
    return jnp.where(pos == 0, nxt, jnp.where(pos == 2, log_f, jnp.where(pos == 3, log_f + prev, 0.0)))


def _hg_head(q_raw, f_raw, v_raw, g_raw, lower, gain, state, tril_ref, masks_ref, side_jobs):
    c = HG_CHUNK
    n_chunks = q_raw.shape[0] // c
    side_jobs = iter(side_jobs)

    def side_job():
        next(side_jobs, lambda: None)()

    def chunk(a, ci):
        return a[ci * c:(ci + 1) * c]

    q = _silu(q_raw)
    forget = lower + (1.0 - lower) * (1.0 / (1.0 + jnp.exp(-f_raw)))
    k = 1.0 - forget
    v = v_raw.astype(BF16)
    log_f = jnp.log(forget)
    hi = log_f.astype(BF16)
    rest = log_f - hi.astype(F32)
    mid = rest.astype(BF16)
    lo = (rest - mid.astype(F32)).astype(BF16)
    terms = jnp.concatenate([hi, mid, lo], axis=1)
    parts = jnp.concatenate([_dot(tril_ref[...], chunk(terms, ci)) for ci in range(n_chunks)], axis=0)
    b = parts[:, :LANES] + parts[:, LANES:2 * LANES] + parts[:, 2 * LANES:]
    b_last = [b[(ci + 1) * c - 1:(ci + 1) * c, :] for ci in range(n_chunks)]

    states = [state]
    for ci in range(n_chunks):
        k_out = (chunk(k, ci) * jnp.exp(b_last[ci] - chunk(b, ci))).astype(BF16)
        states.append(states[ci] * jnp.exp(b_last[ci]) + _dot_tn(chunk(v, ci), k_out))
    side_job()

    q_bf, k_bf = q.astype(BF16), k.astype(BF16)
    scores = [masks_ref[0] * _dot_nt(chunk(q_bf, ci), chunk(k_bf, ci)) for ci in range(n_chunks)]
    for l, h in enumerate(_hg_levels()):
        decay = jnp.exp(_hg_level_exponent(b, log_f, h)).astype(BF16)
        q_l, k_l = q_bf * decay, k_bf * decay
        for ci in range(n_chunks):
            scores[ci] = scores[ci] + masks_ref[1 + l] * _dot_nt(chunk(q_l, ci), chunk(k_l, ci))
        side_job()

    q_in = (q * jnp.exp(b)).astype(BF16)
    outs = [_dot(scores[ci].astype(BF16), chunk(v, ci)) + _dot_nt(chunk(q_in, ci), states[ci].astype(BF16))
            for ci in range(n_chunks)]
    o = _rmsnorm(jnp.concatenate(outs, axis=0), gain) * _silu(g_raw)
    for _ in side_jobs:
        _()
    return o.astype(BF16), states[n_chunks]


def _mix_in_kernel(x_ref, ngain_ref, w_ref, lbl_ref, hgain_ref, tril_ref, masks_ref, sb_ref, o_ref,
                   h_scr, hg_scr, state_ref):
    @pl.when(pl.program_id(1) == 0)
    def _():
        state_ref[...] = jnp.zeros_like(state_ref)

    heads = HG_WIDTH // HG_HEAD_DIM
    sb_cols = sb_ref.shape[1]
    h_scr[...] = _rmsnorm(x_ref[...], ngain_ref[...]).astype(BF16)

    def project_head(h, part):
        def job():
            cols = [sb_cols + (2 * part + j) * HG_WIDTH + h * HG_HEAD_DIM for j in range(2)]
            w = jnp.concatenate([w_ref[:, c0:c0 + HG_HEAD_DIM] for c0 in cols], axis=1)
            hg_scr[h, :, 2 * part * HG_HEAD_DIM:(2 * part + 2) * HG_HEAD_DIM] = _dot(h_scr[...], w)
        return job

    def project_sb(block, width=2 * LANES):
        def job():
            cols = slice(block * width, (block + 1) * width)
            p = _dot(h_scr[...], w_ref[:, cols])
            sb_ref[:, cols] = ((p * SB_Q_SCALE) if block * width < SB_WIDTH else p).astype(BF16)
        return job

    logits = lbl_ref[...]
    e = jnp.exp(logits - jnp.max(logits, axis=0, keepdims=True))
    lower = e[0:1, :] / jnp.sum(e, axis=0, keepdims=True)

    project_head(0, 0)()
    project_head(0, 1)()
    for h in range(heads):
        lanes = slice(h * HG_HEAD_DIM, (h + 1) * HG_HEAD_DIM)
        if h + 1 < heads:
            jobs = [lambda: None, project_head(h + 1, 0), lambda: None, lambda: None, lambda: None, project_head(h + 1, 1)]
        else:
            jobs = [project_sb(blk) for blk in range(sb_cols // (2 * LANES))]
        q, f, v, g = (hg_scr[h, :, j * HG_HEAD_DIM:(j + 1) * HG_HEAD_DIM] for j in range(4))
        o, state_ref[h] = _hg_head(q, f, v, g, lower[:, lanes], hgain_ref[:, lanes], state_ref[h],
                                   tril_ref, masks_ref, jobs)
        o_ref[:, lanes] = o


def _mix_in(x, gain, w_in, lb_logits, hg_gain, batch, seq):
    n, d = x.shape
    sb_cols = 3 * SB_WIDTH
    heads = HG_WIDTH // HG_HEAD_DIM
    tiles = seq // MIX_ROWS
    tril, masks = _hg_constants()
    rows = lambda width: pl.BlockSpec((MIX_ROWS, width), lambda b, t: (b * tiles + t, 0))
    return pl.pallas_call(
        _mix_in_kernel,
        grid=(batch, tiles),
        in_specs=[rows(d), _resident((1, d)), _resident(w_in.shape), _resident(lb_logits.shape),
                  _resident((1, HG_WIDTH)), _resident(tril.shape), _resident(masks.shape)],
        out_specs=[rows(sb_cols), rows(HG_WIDTH)],
        out_shape=[jax.ShapeDtypeStruct((n, sb_cols), BF16), jax.ShapeDtypeStruct((n, HG_WIDTH), BF16)],
        scratch_shapes=[pltpu.VMEM((MIX_ROWS, d), BF16),
                        pltpu.VMEM((heads, MIX_ROWS, 4 * HG_HEAD_DIM), F32),
                        pltpu.VMEM((heads, HG_HEAD_DIM, HG_HEAD_DIM), F32)],
        compiler_params=pltpu.CompilerParams(dimension_semantics=("parallel", "arbitrary"),
                                             vmem_limit_bytes=V7X_VMEM_LIMIT),
        name="mix_in",
    )(x, gain.reshape(1, d), w_in, lb_logits, hg_gain.reshape(1, HG_WIDTH),
      jnp.asarray(tril, BF16), jnp.asarray(masks, F32))


def kernel(x, ffn1_norm, ffn1_w_gate, ffn1_w_up, ffn1_w_down, mix_norm, w_in, sb_out_norm, hg_lower_bound_logits, hg_out_norm, w_out, ffn2_norm, ffn2_w_gate, ffn2_w_up, ffn2_w_down, final_norm):
    batch, seq, d = x.shape
    assert ffn1_norm.shape[0] == 1, "single-layer kernel"
    assert w_in.shape[2] == 3 * SB_WIDTH + 4 * HG_WIDTH and w_out.shape[1] == SB_WIDTH + HG_WIDTH
    assert seq % max(SB_TQ, MIX_ROWS) == 0 and (batch * seq) % FFN_ROWS == 0 and SB_TQ == 2 * SB_TK
    n = batch * seq
    bf = lambda w: w[0].astype(BF16)

    x1, w_in_bf, w_out_bf, wg2, wu2, wd2 = _ffn(
        x.reshape(n, d), ffn1_norm[0], bf(ffn1_w_gate), bf(ffn1_w_up), bf(ffn1_w_down),
        cast=(w_in[0], w_out[0], ffn2_w_gate[0], ffn2_w_up[0], ffn2_w_down[0]))
    sb, o_hg = _mix_in(x1, mix_norm[0], w_in_bf, hg_lower_bound_logits, hg_out_norm[0], batch, seq)
    o_sb = _sb_attention(sb.reshape(batch, seq, -1), sb_out_norm[0], batch, seq)
    out = _ffn(x1, ffn2_norm[0], wg2, wu2, wd2, mix=(o_sb.reshape(n, -1), o_hg, w_out_bf), final_gain=final_norm)
    return out.reshape(batch, seq, d)
```

```python
import functools

import jax
import jax.numpy as jnp
import numpy as np
from jax import lax
from jax.experimental import pallas as pl
from jax.experimental.pallas import tpu as pltpu

F32 = jnp.float32
BF16 = jnp.bfloat16

EPS = 1e-6
LANES = 128
SB_HEAD_DIM = 64
SB_WIDTH = 512
HG_HEAD_DIM = 128
HG_WIDTH = 512
V7X_VMEM_LIMIT = 56 * 1024 * 1024

FFN_ROWS = 512
SB_TQ = 512
SB_TK = 256
SB_Q_SCALE = float(np.log2(np.e)) * SB_HEAD_DIM ** -0.5
SB_DEAD_LOG2 = 160.0
HG_CHUNK = 128
MIX_ROWS = 512


def _dot(a, b):
    return jnp.dot(a, b, preferred_element_type=F32)


def _dot_nt(a, b):
    return lax.dot_general(a, b, (((1,), (1,)), ((), ())), preferred_element_type=F32)


def _dot_tn(a, b):
    return lax.dot_general(a, b, (((0,), (0,)), ((), ())), preferred_element_type=F32)


def _rmsnorm(x, gain):
    return x * lax.rsqrt(jnp.mean(x * x, axis=-1, keepdims=True) + EPS) * gain


def _silu(x):
    return x * (1.0 / (1.0 + jnp.exp(-x)))


def _resident(shape):
    return pl.BlockSpec(shape, lambda *_: (0,) * len(shape), pipeline_mode=pl.Buffered(1))


def _ffn_kernel(*refs, mix_in, final_norm, n_cast):
    refs = list(refs)
    x_ref = refs.pop(0)
    if mix_in:
        osb_ref, ohg_ref, wo_ref = refs[:3]
        del refs[:3]
    gain_ref, wg_ref, wu_ref, wd_ref = refs[:4]
    del refs[:4]
    if final_norm:
        fgain_ref = refs.pop(0)
    cast_in, o_ref, cast_out = refs[:n_cast], refs[n_cast], refs[n_cast + 1:]
    x = x_ref[...]
    if mix_in:
        x = x + _dot(osb_ref[...], wo_ref[:SB_WIDTH, :]) + _dot(ohg_ref[...], wo_ref[SB_WIDTH:, :])
    h = _rmsnorm(x, gain_ref[...]).astype(BF16)
    g = _dot(h, wg_ref[...])
    u = _dot(h, wu_ref[...])
    a = (_silu(g) * u).astype(BF16)
    y = x + 0.5 * _dot(a, wd_ref[...])
    if final_norm:
        y = _rmsnorm(y, fgain_ref[...])
    o_ref[...] = y
    for src, dst in zip(cast_in, cast_out):
        dst[...] = src[...].astype(BF16)


def _slab_spec(rows, cols, steps):
    n_slabs = max(k for k in range(1, steps + 1) if steps % k == 0 and rows % (16 * k) == 0)
    return pl.BlockSpec((rows // n_slabs, cols), lambda i: (i // (steps // n_slabs), 0))


def _ffn(x, gain, wg, wu, wd, *, mix=None, final_gain=None, cast=()):
    n, d = x.shape
    f = wg.shape[1]
    steps = n // FFN_ROWS
    rows = pl.BlockSpec((FFN_ROWS, d), lambda i: (i, 0))
    args, specs = [x], [rows]
    if mix is not None:
        osb, ohg, wo = mix
        args += [osb, ohg, wo]
        specs += [pl.BlockSpec((FFN_ROWS, SB_WIDTH), lambda i: (i, 0)),
                  pl.BlockSpec((FFN_ROWS, HG_WIDTH), lambda i: (i, 0)),
                  _resident(wo.shape)]
    args += [gain.reshape(1, d), wg, wu, wd]
    specs += [_resident((1, d)), _resident((d, f)), _resident((d, f)), _resident((f, d))]
    if final_gain is not None:
        args.append(final_gain.reshape(1, d))
        specs.append(_resident((1, d)))
    cast_specs = [_slab_spec(*w.shape, steps) for w in cast]
    out = pl.pallas_call(
        functools.partial(_ffn_kernel, mix_in=mix is not None, final_norm=final_gain is not None, n_cast=len(cast)),
        grid=(steps,),
        in_specs=specs + cast_specs,
        out_specs=[rows] + cast_specs,
        out_shape=[jax.ShapeDtypeStruct((n, d), F32)] + [jax.ShapeDtypeStruct(w.shape, BF16) for w in cast],
        compiler_params=pltpu.CompilerParams(dimension_semantics=("arbitrary",),
                                             vmem_limit_bytes=V7X_VMEM_LIMIT),
        name="ffn_mix" if mix is not None else "ffn",
    )(*args, *cast)
    return out[0] if not cast else out


def _sb_kernel(q_ref, k_ref, v_ref, gain_ref, tri_ref, o_ref, sp_buf, lb_buf, rs_buf, carry_ref, acc_ref):
    i = pl.program_id(2)
    half = SB_TQ // 2
    lane = lax.broadcasted_iota(jnp.int32, (1, LANES), 1)
    row = lax.broadcasted_iota(jnp.int32, (SB_TK, SB_TK), 0)
    col = lax.broadcasted_iota(jnp.int32, (SB_TK, SB_TK), 1)
    causal = col < row
    q = q_ref[0]
    n_heads = LANES // SB_HEAD_DIM
    in_head = [(lane >= h * SB_HEAD_DIM) & (lane < (h + 1) * SB_HEAD_DIM) for h in range(n_heads)]
    qh = [jnp.where(m, q, 0.0).astype(BF16) for m in in_head]

    def logits(qrows, kb, mask):
        z = _dot_nt(qrows, kb)
        log1p_term = jnp.log2(1.0 + jnp.exp2(-jnp.abs(z)))
        softplus = jnp.maximum(z + log1p_term, log1p_term)
        log_beta = z - softplus
        if mask is not None:
            softplus = jnp.where(mask, softplus, 0.0)
        return softplus, log_beta

    def weights(softplus_bf16, log_beta, carry):
        tail = _dot(softplus_bf16, tri_ref[...])
        return jnp.exp2(log_beta - tail - carry).astype(BF16)

    def rows_of(kb):
        return pl.ds(pl.multiple_of(kb * SB_TK, SB_TK), SB_TK)

    upper, lower, every = slice(0, half), slice(half, SB_TQ), slice(0, SB_TQ)
    blocks = [jnp.maximum(2 * i - 1, 0), 2 * i, 2 * i + 1]
    k_near = [k_ref[0, rows_of(kb), :] for kb in blocks]
    v_near = [v_ref[0, rows_of(kb), :] for kb in blocks]
    no_block = jnp.where(i > 0, 0.0, jnp.inf)
    near = [[(logits(qh[h][rows], k_near[r + 1], causal), logits(qh[h][rows], k_near[r], None))
             for r, rows in enumerate((upper, lower))] for h in range(n_heads)]
    carries = []
    for h in range(n_heads):
        tail = _dot(jnp.concatenate([sp for pair in near[h] for sp, _ in pair], axis=0).astype(BF16), tri_ref[...])
        for r, rows in enumerate((upper, lower)):
            (sp_d, lb_d), (sp_f, lb_f) = near[h][r]
            tail_d, tail_f = tail[2 * r * half:(2 * r + 1) * half], tail[(2 * r + 1) * half:(2 * r + 2) * half]
            carry_d = jnp.sum(sp_d, axis=1, keepdims=True)
            carries.append(carry_d + jnp.sum(sp_f, axis=1, keepdims=True))
            carry_ref[h, rows] = carries[-1]
            w_d = jnp.where(causal, jnp.exp2(lb_d - tail_d), 0.0).astype(BF16)
            w_f = jnp.exp2(lb_f - tail_f - (carry_d + no_block if r == 0 else carry_d)).astype(BF16)
            acc_ref[h, rows] = _dot(jnp.concatenate([w_d, w_f], axis=1),
                                    jnp.concatenate([v_near[r + 1], v_near[r]], axis=0))
    near_decay = jnp.min(functools.reduce(jnp.minimum, carries))

    def stage_logits(slot, kb, rows=every):
        k_blk = k_ref[0, rows_of(kb), :]
        for h in range(n_heads):
            softplus, log_beta = logits(qh[h][rows], k_blk, None)
            sp_buf[slot, h, rows] = softplus.astype(BF16)
            lb_buf[slot, h, rows] = log_beta
            rs_buf[slot, h, rows] = jnp.sum(softplus, axis=1, keepdims=True)

    def stage_values(slot, kb, rows=every):
        v_blk = v_ref[0, rows_of(kb), :]
        for h in range(n_heads):
            carry = carry_ref[h, rows]
            w = weights(sp_buf[slot, h, rows], lb_buf[slot, h, rows], carry)
            acc_ref[h, rows] += _dot(w, v_blk)
            carry_ref[h, rows] = carry + rs_buf[slot, h, rows]

    def decay():
        return jnp.min(carry_ref[...])

    top = 2 * i - 1

    def write_output():
        out = jnp.zeros((SB_TQ, LANES), F32)
        for h in range(n_heads):
            acc = acc_ref[h]
            ms = jnp.sum(jnp.where(in_head[h], acc * acc, 0.0), axis=1, keepdims=True) * (1.0 / SB_HEAD_DIM)
            out = out + jnp.where(in_head[h], acc * lax.rsqrt(ms + EPS), 0.0)
        o_ref[0] = (out * gain_ref[...]).astype(o_ref.dtype)

    write_output()

    @pl.when((i > 0) & (near_decay < SB_DEAD_LOG2))
    def _():
        @pl.when(jnp.min(carry_ref[:, lower]) < SB_DEAD_LOG2)
        def _():
            stage_logits(0, top, lower)
            stage_values(0, top, lower)

        @pl.when(decay() < SB_DEAD_LOG2)
        def _():
            stage_logits(1, top - 1)
            stage_values(1, top - 1)

        @pl.when((i > 1) & (decay() < SB_DEAD_LOG2))
        def _():
            first = top - 2
            stage_logits(0, first)

            def body(state):
                p, _ = state
                swept = decay()
                b0 = first - 2 * p
                stage_logits(1, b0 - 1)
                stage_values(0, b0)
                stage_logits(0, b0 - 2)
                stage_values(1, b0 - 1)
                return p + 1, swept

            _, reached = lax.while_loop(lambda state: (state[0] < i - 2) & (state[1] < SB_DEAD_LOG2), body,
                                        (jnp.int32(0), decay()))

            @pl.when(reached < SB_DEAD_LOG2)
            def _():
                stage_logits(1, 0)
                stage_values(0, 1)
                stage_values(1, 0)

        write_output()


def _sb_attention(qkv, gain, batch, seq):
    pairs = SB_WIDTH // LANES
    n_heads = LANES // SB_HEAD_DIM
    tri = np.tril(np.ones((SB_TK, SB_TK), np.float32), -1)
    return pl.pallas_call(
        _sb_kernel,
        grid=(batch, pairs, seq // SB_TQ),
        in_specs=[pl.BlockSpec((1, SB_TQ, LANES), lambda b, p, i: (b, i, p)),
                  pl.BlockSpec((1, seq, LANES), lambda b, p, i: (b, 0, pairs + p)),
                  pl.BlockSpec((1, seq, LANES), lambda b, p, i: (b, 0, 2 * pairs + p)),
                  pl.BlockSpec((1, LANES), lambda b, p, i: (0, p)),
                  _resident((SB_TK, SB_TK))],
        out_specs=pl.BlockSpec((1, SB_TQ, LANES), lambda b, p, i: (b, i, p)),
        out_shape=jax.ShapeDtypeStruct((batch, seq, SB_WIDTH), BF16),
        scratch_shapes=[pltpu.VMEM((2, n_heads, SB_TQ, SB_TK), BF16),
                        pltpu.VMEM((2, n_heads, SB_TQ, SB_TK), F32),
                        pltpu.VMEM((2, n_heads, SB_TQ, 1), F32),
                        pltpu.VMEM((n_heads, SB_TQ, 1), F32),
                        pltpu.VMEM((n_heads, SB_TQ, LANES), F32)],
        compiler_params=pltpu.CompilerParams(dimension_semantics=("parallel", "parallel", "parallel"),
                                             vmem_limit_bytes=V7X_VMEM_LIMIT),
        name="sb_attn",
    )(qkv, qkv, qkv, gain.reshape(1, SB_WIDTH), jnp.asarray(tri, BF16))


def _hg_levels():
    return [HG_CHUNK >> (l + 1) for l in range(HG_CHUNK.bit_length() - 1)]


def _hg_constants():
    c = HG_CHUNK
    t = np.arange(c)[:, None]
    j = np.arange(c)[None, :]
    masks = [(t == j)]
    for h in _hg_levels():
        masks.append((t // (2 * h) == j // (2 * h)) & (t % (2 * h) >= h) & (j % (2 * h) < h))
    return (j <= t).astype(np.float32), np.stack(masks).astype(np.float32)


def _hg_level_exponent(b, log_f, h):
    c = b.shape[0]
    if h >= 4:
        blocks = b.reshape(c // (2 * h), 2 * h, LANES)
        mid = jnp.broadcast_to(blocks[:, h - 1:h, :], blocks.shape).reshape(c, LANES)
        return -jnp.abs(b - mid)
    pos = lax.broadcasted_iota(jnp.int32, (c, LANES), 0) % (2 * h)
    if h == 1:
        return jnp.where(pos == 1, log_f, 0.0)
    prev = pltpu.roll(log_f, 1, 0)
    nxt = pltpu.roll(log_f, c - 1, 0)
    return jnp.where(pos == 0, nxt, jnp.where(pos == 2, log_f, jnp.where(pos == 3, log_f + prev, 0.0)))


def _hg_head(q_raw, f_raw, v_raw, g_raw, lower, gain, state, tril_ref, masks_ref, side_jobs):
    c = HG_CHUNK
    n_chunks = q_raw.shape[0] // c
    side_jobs = iter(side_jobs)

    def side_job():
        next(side_jobs, lambda: None)()

    def chunk(a, ci):
        return a[ci * c:(ci + 1) * c]

    q = _silu(q_raw)
    forget = lower + (1.0 - lower) * (1.0 / (1.0 + jnp.exp(-f_raw)))
    k = 1.0 - forget
    v = v_raw.astype(BF16)
    log_f = jnp.log(forget)
    hi = log_f.astype(BF16)
    rest = log_f - hi.astype(F32)
    mid = rest.astype(BF16)
    lo = (rest - mid.astype(F32)).astype(BF16)
    terms = jnp.concatenate([hi, mid, lo], axis=1)
    parts = jnp.concatenate([_dot(tril_ref[...], chunk(terms, ci)) for ci in range(n_chunks)], axis=0)
    b = parts[:, :LANES] + parts[:, LANES:2 * LANES] + parts[:, 2 * LANES:]
    b_last = [b[(ci + 1) * c - 1:(ci + 1) * c, :] for ci in range(n_chunks)]

    states = [state]
    for ci in range(n_chunks):
        k_out = (chunk(k, ci) * jnp.exp(b_last[ci] - chunk(b, ci))).astype(BF16)
        states.append(states[ci] * jnp.exp(b_last[ci]) + _dot_tn(chunk(v, ci), k_out))
    side_job()

    q_bf, k_bf = q.astype(BF16), k.astype(BF16)
    scores = [masks_ref[0] * _dot_nt(chunk(q_bf, ci), chunk(k_bf, ci)) for ci in range(n_chunks)]
    for l, h in enumerate(_hg_levels()):
        decay = jnp.exp(_hg_level_exponent(b, log_f, h)).astype(BF16)
        q_l, k_l = q_bf * decay, k_bf * decay
        for ci in range(n_chunks):
            scores[ci] = scores[ci] + masks_ref[1 + l] * _dot_nt(chunk(q_l, ci), chunk(k_l, ci))
        side_job()

    q_in = (q * jnp.exp(b)).astype(BF16)
    outs = [_dot(jnp.concatenate([scores[ci].astype(BF16), chunk(q_in, ci)], axis=1),
                 jnp.concatenate([chunk(v, ci), states[ci].astype(BF16).T], axis=0)) for ci in range(n_chunks)]
    o = _rmsnorm(jnp.concatenate(outs, axis=0), gain) * _silu(g_raw)
    for _ in side_jobs:
        _()
    return o.astype(BF16), states[n_chunks]


def _mix_in_kernel(x_ref, ngain_ref, w_ref, lbl_ref, hgain_ref, tril_ref, masks_ref, sb_ref, o_ref,
                   h_scr, hg_scr, state_ref):
    @pl.when(pl.program_id(1) == 0)
    def _():
        state_ref[...] = jnp.zeros_like(state_ref)

    heads = HG_WIDTH // HG_HEAD_DIM
    sb_cols = sb_ref.shape[1]
    h_scr[...] = _rmsnorm(x_ref[...], ngain_ref[...]).astype(BF16)

    def project_head(h, part):
        def job():
            cols = [sb_cols + (2 * part + j) * HG_WIDTH + h * HG_HEAD_DIM for j in range(2)]
            w = jnp.concatenate([w_ref[:, c0:c0 + HG_HEAD_DIM] for c0 in cols], axis=1)
            hg_scr[h, :, 2 * part * HG_HEAD_DIM:(2 * part + 2) * HG_HEAD_DIM] = _dot(h_scr[...], w)
        return job

    def project_sb(block, width=2 * LANES):
        def job():
            cols = slice(block * width, (block + 1) * width)
            p = _dot(h_scr[...], w_ref[:, cols])
            sb_ref[:, cols] = ((p * SB_Q_SCALE) if block * width < SB_WIDTH else p).astype(BF16)
        return job

    logits = lbl_ref[...]
    e = jnp.exp(logits - jnp.max(logits, axis=0, keepdims=True))
    lower = e[0:1, :] / jnp.sum(e, axis=0, keepdims=True)

    project_head(0, 0)()
    project_head(0, 1)()
    for h in range(heads):
        lanes = slice(h * HG_HEAD_DIM, (h + 1) * HG_HEAD_DIM)
        if h + 1 < heads:
            jobs = [lambda: None, project_head(h + 1, 0), lambda: None, lambda: None, lambda: None, project_head(h + 1, 1)]
        else:
            jobs = [project_sb(blk) for blk in range(sb_cols // (2 * LANES))]
        q, f, v, g = (hg_scr[h, :, j * HG_HEAD_DIM:(j + 1) * HG_HEAD_DIM] for j in range(4))
        o, state_ref[h] = _hg_head(q, f, v, g, lower[:, lanes], hgain_ref[:, lanes], state_ref[h],
                                   tril_ref, masks_ref, jobs)
        o_ref[:, lanes] = o


def _mix_in(x, gain, w_in, lb_logits, hg_gain, batch, seq):
    n, d = x.shape
    sb_cols = 3 * SB_WIDTH
    heads = HG_WIDTH // HG_HEAD_DIM
    tiles = seq // MIX_ROWS
    tril, masks = _hg_constants()
    rows = lambda width: pl.BlockSpec((MIX_ROWS, width), lambda b, t: (b * tiles + t, 0))
    return pl.pallas_call(
        _mix_in_kernel,
        grid=(batch, tiles),
        in_specs=[rows(d), _resident((1, d)), _resident(w_in.shape), _resident(lb_logits.shape),
                  _resident((1, HG_WIDTH)), _resident(tril.shape), _resident(masks.shape)],
        out_specs=[rows(sb_cols), rows(HG_WIDTH)],
        out_shape=[jax.ShapeDtypeStruct((n, sb_cols), BF16), jax.ShapeDtypeStruct((n, HG_WIDTH), BF16)],
        scratch_shapes=[pltpu.VMEM((MIX_ROWS, d), BF16),
                        pltpu.VMEM((heads, MIX_ROWS, 4 * HG_HEAD_DIM), F32),
                        pltpu.VMEM((heads, HG_HEAD_DIM, HG_HEAD_DIM), F32)],
        compiler_params=pltpu.CompilerParams(dimension_semantics=("parallel", "arbitrary"),
                                             vmem_limit_bytes=V7X_VMEM_LIMIT),
        name="mix_in",
    )(x, gain.reshape(1, d), w_in, lb_logits, hg_gain.reshape(1, HG_WIDTH),
      jnp.asarray(tril, BF16), jnp.asarray(masks, F32))


def kernel(x, ffn1_norm, ffn1_w_gate, ffn1_w_up, ffn1_w_down, mix_norm, w_in, sb_out_norm, hg_lower_bound_logits, hg_out_norm, w_out, ffn2_norm, ffn2_w_gate, ffn2_w_up, ffn2_w_down, final_norm):
    batch, seq, d = x.shape
    assert ffn1_norm.shape[0] == 1, "single-layer kernel"
    assert w_in.shape[2] == 3 * SB_WIDTH + 4 * HG_WIDTH and w_out.shape[1] == SB_WIDTH + HG_WIDTH
    assert seq % max(SB_TQ, MIX_ROWS) == 0 and (batch * seq) % FFN_ROWS == 0 and SB_TQ == 2 * SB_TK
    n = batch * seq
    bf = lambda w: w[0].astype(BF16)

    x1, w_in_bf, w_out_bf, wg2, wu2, wd2 = _ffn(
        x.reshape(n, d), ffn1_norm[0], bf(ffn1_w_gate), bf(ffn1_w_up), bf(ffn1_w_down),
        cast=(w_in[0], w_out[0], ffn2_w_gate[0], ffn2_w_up[0], ffn2_w_down[0]))
    sb, o_hg = _mix_in(x1, mix_norm[0], w_in_bf, hg_lower_bound_logits, hg_out_norm[0], batch, seq)
    o_sb = _sb_attention(sb.reshape(batch, seq, -1), sb_out_norm[0], batch, seq)
    out = _ffn(x1, ffn2_norm[0], wg2, wu2, wd2, mix=(o_sb.reshape(n, -1), o_hg, w_out_bf), final_gain=final_norm)
    return out.reshape(batch, seq, d)
```

```python
import functools

import jax
import jax.numpy as jnp
import numpy as np
from jax import lax
from jax.experimental import pallas as pl
from jax.experimental.pallas import tpu as pltpu

F32 = jnp.float32
BF16 = jnp.bfloat16

EPS = 1e-6
LANES = 128
SB_HEAD_DIM = 64
SB_WIDTH = 512
HG_HEAD_DIM = 128
HG_WIDTH = 512
V7X_VMEM_LIMIT = 56 * 1024 * 1024

FFN_ROWS = 512
SB_TQ = 512
SB_TK = 256
SB_Q_SCALE = float(np.log2(np.e)) * SB_HEAD_DIM ** -0.5
SB_DEAD_LOG2 = 160.0
HG_CHUNK = 128
MIX_ROWS = 512


def _dot(a, b):
    return jnp.dot(a, b, preferred_element_type=F32)


def _dot_nt(a, b):
    return lax.dot_general(a, b, (((1,), (1,)), ((), ())), preferred_element_type=F32)


def _dot_tn(a, b):
    return lax.dot_general(a, b, (((0,), (0,)), ((), ())), preferred_element_type=F32)


def _rmsnorm(x, gain):
    return x * lax.rsqrt(jnp.mean(x * x, axis=-1, keepdims=True) + EPS) * gain


def _silu(x):
    return x * (1.0 / (1.0 + jnp.exp(-x)))


def _resident(shape):
    return pl.BlockSpec(shape, lambda *_: (0,) * len(shape), pipeline_mode=pl.Buffered(1))


def _ffn_kernel(*refs, mix_in, final_norm, n_cast):
    refs = list(refs)
    x_ref = refs.pop(0)
    if mix_in:
        osb_ref, ohg_ref, wo_ref = refs[:3]
        del refs[:3]
    gain_ref, wg_ref, wu_ref, wd_ref = refs[:4]
    del refs[:4]
    if final_norm:
        fgain_ref = refs.pop(0)
    cast_in, o_ref, cast_out = refs[:n_cast], refs[n_cast], refs[n_cast + 1:]
    x = x_ref[...]
    if mix_in:
        x = x + _dot(osb_ref[...], wo_ref[:SB_WIDTH, :]) + _dot(ohg_ref[...], wo_ref[SB_WIDTH:, :])
    h = _rmsnorm(x, gain_ref[...]).astype(BF16)
    g = _dot(h, wg_ref[...])
    u = _dot(h, wu_ref[...])
    a = (_silu(g) * u).astype(BF16)
    y = x + 0.5 * _dot(a, wd_ref[...])
    if final_norm:
        y = _rmsnorm(y, fgain_ref[...])
    o_ref[...] = y
    for src, dst in zip(cast_in, cast_out):
        dst[...] = src[...].astype(BF16)


def _slab_spec(rows, cols, steps):
    n_slabs = max(k for k in range(1, steps + 1) if steps % k == 0 and rows % (16 * k) == 0)
    return pl.BlockSpec((rows // n_slabs, cols), lambda i: (i // (steps // n_slabs), 0))


def _ffn(x, gain, wg, wu, wd, *, mix=None, final_gain=None, cast=()):
    n, d = x.shape
    f = wg.shape[1]
    steps = n // FFN_ROWS
    rows = pl.BlockSpec((FFN_ROWS, d), lambda i: (i, 0))
    args, specs = [x], [rows]
    if mix is not None:
        osb, ohg, wo = mix
        args += [osb, ohg, wo]
        specs += [pl.BlockSpec((FFN_ROWS, SB_WIDTH), lambda i: (i, 0)),
                  pl.BlockSpec((FFN_ROWS, HG_WIDTH), lambda i: (i, 0)),
                  _resident(wo.shape)]
    args += [gain.reshape(1, d), wg, wu, wd]
    specs += [_resident((1, d)), _resident((d, f)), _resident((d, f)), _resident((f, d))]
    if final_gain is not None:
        args.append(final_gain.reshape(1, d))
        specs.append(_resident((1, d)))
    cast_specs = [_slab_spec(*w.shape, steps) for w in cast]
    out = pl.pallas_call(
        functools.partial(_ffn_kernel, mix_in=mix is not None, final_norm=final_gain is not None, n_cast=len(cast)),
        grid=(steps,),
        in_specs=specs + cast_specs,
        out_specs=[rows] + cast_specs,
        out_shape=[jax.ShapeDtypeStruct((n, d), F32)] + [jax.ShapeDtypeStruct(w.shape, BF16) for w in cast],
        compiler_params=pltpu.CompilerParams(dimension_semantics=("arbitrary",),
                                             vmem_limit_bytes=V7X_VMEM_LIMIT),
        name="ffn_mix" if mix is not None else "ffn",
    )(*args, *cast)
    return out[0] if not cast else out


def _sb_kernel(q_ref, k_ref, v_ref, gain_ref, tri_ref, o_ref, sp_buf, lb_buf, rs_buf, carry_ref, acc_ref):
    i = pl.program_id(2)
    half = SB_TQ // 2
    lane = lax.broadcasted_iota(jnp.int32, (1, LANES), 1)
    row = lax.broadcasted_iota(jnp.int32, (SB_TK, SB_TK), 0)
    col = lax.broadcasted_iota(jnp.int32, (SB_TK, SB_TK), 1)
    causal = col < row
    q = q_ref[0]
    n_heads = LANES // SB_HEAD_DIM
    in_head = [(lane >= h * SB_HEAD_DIM) & (lane < (h + 1) * SB_HEAD_DIM) for h in range(n_heads)]
    qh = [jnp.where(m, q, 0.0).astype(BF16) for m in in_head]

    def logits(qrows, kb, mask):
        z = _dot_nt(qrows, kb)
        log1p_term = jnp.log2(1.0 + jnp.exp2(-jnp.abs(z)))
        softplus = jnp.maximum(z + log1p_term, log1p_term)
        log_beta = z - softplus
        if mask is not None:
            softplus = jnp.where(mask, softplus, 0.0)
        return softplus, log_beta

    def weights(softplus_bf16, log_beta, carry):
        tail = _dot(softplus_bf16, tri_ref[...])
        return jnp.exp2(log_beta - tail - carry).astype(BF16)

    def rows_of(kb):
        return pl.ds(pl.multiple_of(kb * SB_TK, SB_TK), SB_TK)

    upper, lower, every = slice(0, half), slice(half, SB_TQ), slice(0, SB_TQ)
    blocks = [jnp.maximum(2 * i - 1, 0), 2 * i, 2 * i + 1]
    k_near = [k_ref[0, rows_of(kb), :] for kb in blocks]
    v_near = [v_ref[0, rows_of(kb), :] for kb in blocks]
    no_block = jnp.where(i > 0, 0.0, jnp.inf)
    near = [[(logits(qh[h][rows], k_near[r + 1], causal), logits(qh[h][rows], k_near[r], None))
             for r, rows in enumerate((upper, lower))] for h in range(n_heads)]
    carries = []
    for h in range(n_heads):
        tail = _dot(jnp.concatenate([sp for pair in near[h] for sp, _ in pair], axis=0).astype(BF16), tri_ref[...])
        for r, rows in enumerate((upper, lower)):
            (sp_d, lb_d), (sp_f, lb_f) = near[h][r]
            tail_d, tail_f = tail[2 * r * half:(2 * r + 1) * half], tail[(2 * r + 1) * half:(2 * r + 2) * half]
            carry_d = jnp.sum(sp_d, axis=1, keepdims=True)
            carries.append(carry_d + jnp.sum(sp_f, axis=1, keepdims=True))
            carry_ref[h, rows] = carries[-1]
            w_d = jnp.where(causal, jnp.exp2(lb_d - tail_d), 0.0).astype(BF16)
            w_f = jnp.exp2(lb_f - tail_f - (carry_d + no_block if r == 0 else carry_d)).astype(BF16)
            acc_ref[h, rows] = _dot(jnp.concatenate([w_d, w_f], axis=1),
                                    jnp.concatenate([v_near[r + 1], v_near[r]], axis=0))
    near_decay = jnp.min(functools.reduce(jnp.minimum, carries))

    def stage_logits(slot, kb, rows=every):
        k_blk = k_ref[0, rows_of(kb), :]
        for h in range(n_heads):
            softplus, log_beta = logits(qh[h][rows], k_blk, None)
            sp_buf[slot, h, rows] = softplus.astype(BF16)
            lb_buf[slot, h, rows] = log_beta
            rs_buf[slot, h, rows] = jnp.sum(softplus, axis=1, keepdims=True)

    def stage_values(slot, kb, rows=every):
        v_blk = v_ref[0, rows_of(kb), :]
        for h in range(n_heads):
            carry = carry_ref[h, rows]
            w = weights(sp_buf[slot, h, rows], lb_buf[slot, h, rows], carry)
            acc_ref[h, rows] += _dot(w, v_blk)
            carry_ref[h, rows] = carry + rs_buf[slot, h, rows]

    def decay():
        return jnp.min(carry_ref[...])

    top = 2 * i - 1

    def write_output():
        out = jnp.zeros((SB_TQ, LANES), F32)
        for h in range(n_heads):
            acc = acc_ref[h]
            ms = jnp.sum(jnp.where(in_head[h], acc * acc, 0.0), axis=1, keepdims=True) * (1.0 / SB_HEAD_DIM)
            out = out + jnp.where(in_head[h], acc * lax.rsqrt(ms + EPS), 0.0)
        o_ref[0] = (out * gain_ref[...]).astype(o_ref.dtype)

    write_output()

    @pl.when((i > 0) & (near_decay < SB_DEAD_LOG2))
    def _():
        @pl.when(jnp.min(carry_ref[:, lower]) < SB_DEAD_LOG2)
        def _():
            stage_logits(0, top, lower)
            stage_values(0, top, lower)

        @pl.when(decay() < SB_DEAD_LOG2)
        def _():
            stage_logits(1, top - 1)
            stage_values(1, top - 1)

        @pl.when((i > 1) & (decay() < SB_DEAD_LOG2))
        def _():
            first = top - 2
            stage_logits(0, first)

            def body(state):
                p, _ = state
                swept = decay()
                b0 = first - 2 * p
                stage_logits(1, b0 - 1)
                stage_values(0, b0)
                stage_logits(0, b0 - 2)
                stage_values(1, b0 - 1)
                return p + 1, swept

            _, reached = lax.while_loop(lambda state: (state[0] < i - 2) & (state[1] < SB_DEAD_LOG2), body,
                                        (jnp.int32(0), decay()))

            @pl.when(reached < SB_DEAD_LOG2)
            def _():
                stage_logits(1, 0)
                stage_values(0, 1)
                stage_values(1, 0)

        write_output()


def _sb_attention(qkv, gain, batch, seq):
    pairs = SB_WIDTH // LANES
    n_heads = LANES // SB_HEAD_DIM
    tri = np.tril(np.ones((SB_TK, SB_TK), np.float32), -1)
    return pl.pallas_call(
        _sb_kernel,
        grid=(batch, pairs, seq // SB_TQ),
        in_specs=[pl.BlockSpec((1, SB_TQ, LANES), lambda b, p, i: (b, i, p)),
                  pl.BlockSpec((1, seq, LANES), lambda b, p, i: (b, 0, pairs + p)),
                  pl.BlockSpec((1, seq, LANES), lambda b, p, i: (b, 0, 2 * pairs + p)),
                  pl.BlockSpec((1, LANES), lambda b, p, i: (0, p)),
                  _resident((SB_TK, SB_TK))],
        out_specs=pl.BlockSpec((1, SB_TQ, LANES), lambda b, p, i: (b, i, p)),
        out_shape=jax.ShapeDtypeStruct((batch, seq, SB_WIDTH), BF16),
        scratch_shapes=[pltpu.VMEM((2, n_heads, SB_TQ, SB_TK), BF16),
                        pltpu.VMEM((2, n_heads, SB_TQ, SB_TK), F32),
                        pltpu.VMEM((2, n_heads, SB_TQ, 1), F32),
                        pltpu.VMEM((n_heads, SB_TQ, 1), F32),
                        pltpu.VMEM((n_heads, SB_TQ, LANES), F32)],
        compiler_params=pltpu.CompilerParams(dimension_semantics=("parallel", "parallel", "parallel"),
                                             vmem_limit_bytes=V7X_VMEM_LIMIT),
        name="sb_attn",
    )(qkv, qkv, qkv, gain.reshape(1, SB_WIDTH), jnp.asarray(tri, BF16))


def _hg_levels():
    return [HG_CHUNK >> (l + 1) for l in range(HG_CHUNK.bit_length() - 1)]


def _hg_constants():
    c = HG_CHUNK
    t = np.arange(c)[:, None]
    j = np.arange(c)[None, :]
    masks = [(t == j)]
    for h in _hg_levels():
        masks.append((t // (2 * h) == j // (2 * h)) & (t % (2 * h) >= h) & (j % (2 * h) < h))
    tril = (j <= t).astype(np.float32)
    return np.concatenate([tril, tril], axis=1), np.stack(masks).astype(np.float32)


def _hg_level_exponent(b, log_f, h):
    c = b.shape[0]
    if h >= 4:
        blocks = b.reshape(c // (2 * h), 2 * h, LANES)
        mid = jnp.broadcast_to(blocks[:, h - 1:h, :], blocks.shape).reshape(c, LANES)
        return -jnp.abs(b - mid)
    pos = lax.broadcasted_iota(jnp.int32, (c, LANES), 0) % (2 * h)
    if h == 1:
        return jnp.where(pos == 1, log_f, 0.0)
    prev = pltpu.roll(log_f, 1, 0)
    nxt = pltpu.roll(log_f, c - 1, 0)
    return jnp.where(pos == 0, nxt, jnp.where(pos == 2, log_f, jnp.where(pos == 3, log_f + prev, 0.0)))


def _hg_head(q_raw, f_raw, v_raw, g_raw, lower, gain, state, tril_ref, masks_ref, side_jobs):
    c = HG_CHUNK
    n_chunks = q_raw.shape[0] // c
    side_jobs = iter(side_jobs)

    def side_job():
        next(side_jobs, lambda: None)()

    def chunk(a, ci):
        return a[ci * c:(ci + 1) * c]

    q = _silu(q_raw)
    forget = lower + (1.0 - lower) * (1.0 / (1.0 + jnp.exp(-f_raw)))
    k = 1.0 - forget
    v = v_raw.astype(BF16)
    log_f = jnp.log(forget)
    hi = log_f.astype(BF16)
    rest = log_f - hi.astype(F32)
    mid = rest.astype(BF16)
    lo = (rest - mid.astype(F32)).astype(BF16)
    top = jnp.concatenate([hi, lo], axis=1)
    bottom = jnp.concatenate([mid, jnp.zeros_like(mid)], axis=1)
    parts = jnp.concatenate([_dot(tril_ref[...], jnp.concatenate([chunk(top, ci), chunk(bottom, ci)], axis=0))
                             for ci in range(n_chunks)], axis=0)
    b = parts[:, :LANES] + parts[:, LANES:]
    b_last = [b[(ci + 1) * c - 1:(ci + 1) * c, :] for ci in range(n_chunks)]

    states = [state]
    for ci in range(n_chunks):
        k_out = (chunk(k, ci) * jnp.exp(b_last[ci] - chunk(b, ci))).astype(BF16)
        states.append(states[ci] * jnp.exp(b_last[ci]) + _dot_tn(chunk(v, ci), k_out))
    side_job()

    q_bf, k_bf = q.astype(BF16), k.astype(BF16)
    scores = [masks_ref[0] * _dot_nt(chunk(q_bf, ci), chunk(k_bf, ci)) for ci in range(n_chunks)]
    for l, h in enumerate(_hg_levels()):
        decay = jnp.exp(_hg_level_exponent(b, log_f, h)).astype(BF16)
        q_l, k_l = q_bf * decay, k_bf * decay
        for ci in range(n_chunks):
            scores[ci] = scores[ci] + masks_ref[1 + l] * _dot_nt(chunk(q_l, ci), chunk(k_l, ci))
        side_job()

    q_in = (q * jnp.exp(b)).astype(BF16)
    outs = [_dot(jnp.concatenate([scores[ci].astype(BF16), chunk(q_in, ci)], axis=1),
                 jnp.concatenate([chunk(v, ci), states[ci].astype(BF16).T], axis=0)) for ci in range(n_chunks)]
    o = _rmsnorm(jnp.concatenate(outs, axis=0), gain) * _silu(g_raw)
    for _ in side_jobs:
        _()
    return o.astype(BF16), states[n_chunks]


def _mix_in_kernel(x_ref, ngain_ref, w_ref, lbl_ref, hgain_ref, tril_ref, masks_ref, sb_ref, o_ref,
                   h_scr, hg_scr, state_ref):
    @pl.when(pl.program_id(1) == 0)
    def _():
        state_ref[...] = jnp.zeros_like(state_ref)

    heads = HG_WIDTH // HG_HEAD_DIM
    sb_cols = sb_ref.shape[1]
    h_scr[...] = _rmsnorm(x_ref[...], ngain_ref[...]).astype(BF16)

    def project_head(h, part):
        def job():
            cols = [sb_cols + (2 * part + j) * HG_WIDTH + h * HG_HEAD_DIM for j in range(2)]
            w = jnp.concatenate([w_ref[:, c0:c0 + HG_HEAD_DIM] for c0 in cols], axis=1)
            hg_scr[h, :, 2 * part * HG_HEAD_DIM:(2 * part + 2) * HG_HEAD_DIM] = _dot(h_scr[...], w)
        return job

    def project_sb(block, width=2 * LANES):
        def job():
            cols = slice(block * width, (block + 1) * width)
            p = _dot(h_scr[...], w_ref[:, cols])
            sb_ref[:, cols] = ((p * SB_Q_SCALE) if block * width < SB_WIDTH else p).astype(BF16)
        return job

    logits = lbl_ref[...]
    e = jnp.exp(logits - jnp.max(logits, axis=0, keepdims=True))
    lower = e[0:1, :] / jnp.sum(e, axis=0, keepdims=True)

    project_head(0, 0)()
    project_head(0, 1)()
    for h in range(heads):
        lanes = slice(h * HG_HEAD_DIM, (h + 1) * HG_HEAD_DIM)
        if h + 1 < heads:
            jobs = [lambda: None, project_head(h + 1, 0), lambda: None, lambda: None, lambda: None, project_head(h + 1, 1)]
        else:
            jobs = [project_sb(blk) for blk in range(sb_cols // (2 * LANES))]
        q, f, v, g = (hg_scr[h, :, j * HG_HEAD_DIM:(j + 1) * HG_HEAD_DIM] for j in range(4))
        o, state_ref[h] = _hg_head(q, f, v, g, lower[:, lanes], hgain_ref[:, lanes], state_ref[h],
                                   tril_ref, masks_ref, jobs)
        o_ref[:, lanes] = o


def _mix_in(x, gain, w_in, lb_logits, hg_gain, batch, seq):
    n, d = x.shape
    sb_cols = 3 * SB_WIDTH
    heads = HG_WIDTH // HG_HEAD_DIM
    tiles = seq // MIX_ROWS
    tril, masks = _hg_constants()
    rows = lambda width: pl.BlockSpec((MIX_ROWS, width), lambda b, t: (b * tiles + t, 0))
    return pl.pallas_call(
        _mix_in_kernel,
        grid=(batch, tiles),
        in_specs=[rows(d), _resident((1, d)), _resident(w_in.shape), _resident(lb_logits.shape),
                  _resident((1, HG_WIDTH)), _resident(tril.shape), _resident(masks.shape)],
        out_specs=[rows(sb_cols), rows(HG_WIDTH)],
        out_shape=[jax.ShapeDtypeStruct((n, sb_cols), BF16), jax.ShapeDtypeStruct((n, HG_WIDTH), BF16)],
        scratch_shapes=[pltpu.VMEM((MIX_ROWS, d), BF16),
                        pltpu.VMEM((heads, MIX_ROWS, 4 * HG_HEAD_DIM), F32),
                        pltpu.VMEM((heads, HG_HEAD_DIM, HG_HEAD_DIM), F32)],
        compiler_params=pltpu.CompilerParams(dimension_semantics=("parallel", "arbitrary"),
                                             vmem_limit_bytes=V7X_VMEM_LIMIT),
        name="mix_in",
    )(x, gain.reshape(1, d), w_in, lb_logits, hg_gain.reshape(1, HG_WIDTH),
      jnp.asarray(tril, BF16), jnp.asarray(masks, F32))


def kernel(x, ffn1_norm, ffn1_w_gate, ffn1_w_up, ffn1_w_down, mix_norm, w_in, sb_out_norm, hg_lower_bound_logits, hg_out_norm, w_out, ffn2_norm, ffn2_w_gate, ffn2_w_up, ffn2_w_down, final_norm):
    batch, seq, d = x.shape
    assert ffn1_norm.shape[0] == 1, "single-layer kernel"
    assert w_in.shape[2] == 3 * SB_WIDTH + 4 * HG_WIDTH and w_out.shape[1] == SB_WIDTH + HG_WIDTH
    assert seq % max(SB_TQ, MIX_ROWS) == 0 and (batch * seq) % FFN_ROWS == 0 and SB_TQ == 2 * SB_TK
    n = batch * seq
    bf = lambda w: w[0].astype(BF16)

    x1, w_in_bf, w_out_bf, wg2, wu2, wd2 = _ffn(
        x.reshape(n, d), ffn1_norm[0], bf(ffn1_w_gate), bf(ffn1_w_up), bf(ffn1_w_down),
        cast=(w_in[0], w_out[0], ffn2_w_gate[0], ffn2_w_up[0], ffn2_w_down[0]))
    sb, o_hg = _mix_in(x1, mix_norm[0], w_in_bf, hg_lower_bound_logits, hg_out_norm[0], batch, seq)
    o_sb = _sb_attention(sb.reshape(batch, seq, -1), sb_out_norm[0], batch, seq)
    out = _ffn(x1, ffn2_norm[0], wg2, wu2, wd2, mix=(o_sb.reshape(n, -1), o_hg, w_out_bf), final_gain=final_norm)
    return out.reshape(batch, seq, d)
```

```python
import functools

import jax
import jax.numpy as jnp
import numpy as np
from jax import lax
from jax.experimental import pallas as pl
from jax.experimental.pallas import tpu as pltpu

F32 = jnp.float32
BF16 = jnp.bfloat16

EPS = 1e-6
LANES = 128
SB_HEAD_DIM = 64
SB_WIDTH = 512
HG_HEAD_DIM = 128
HG_WIDTH = 512
V7X_VMEM_LIMIT = 56 * 1024 * 1024

FFN_ROWS = 512
SB_TQ = 512
SB_TK = 256
SB_Q_SCALE = float(np.log2(np.e)) * SB_HEAD_DIM ** -0.5
SB_DEAD_LOG2 = 160.0
HG_CHUNK = 128
MIX_ROWS = 1024


def _dot(a, b):
    return jnp.dot(a, b, preferred_element_type=F32)


def _dot_nt(a, b):
    return lax.dot_general(a, b, (((1,), (1,)), ((), ())), preferred_element_type=F32)


def _dot_tn(a, b):
    return lax.dot_general(a, b, (((0,), (0,)), ((), ())), preferred_element_type=F32)


def _rmsnorm(x, gain):
    return x * lax.rsqrt(jnp.mean(x * x, axis=-1, keepdims=True) + EPS) * gain


def _silu(x):
    return x * (1.0 / (1.0 + jnp.exp(-x)))


def _resident(shape):
    return pl.BlockSpec(shape, lambda *_: (0,) * len(shape), pipeline_mode=pl.Buffered(1))


def _ffn_kernel(*refs, mix_in, final_norm, n_cast):
    refs = list(refs)
    x_ref = refs.pop(0)
    if mix_in:
        osb_ref, ohg_ref, wo_ref = refs[:3]
        del refs[:3]
    gain_ref, wg_ref, wu_ref, wd_ref = refs[:4]
    del refs[:4]
    if final_norm:
        fgain_ref = refs.pop(0)
    cast_in, o_ref, cast_out = refs[:n_cast], refs[n_cast], refs[n_cast + 1:]
    x = x_ref[...]
    if mix_in:
        x = x + _dot(osb_ref[...], wo_ref[:SB_WIDTH, :]) + _dot(ohg_ref[...], wo_ref[SB_WIDTH:, :])
    h = _rmsnorm(x, gain_ref[...]).astype(BF16)
    g = _dot(h, wg_ref[...])
    u = _dot(h, wu_ref[...])
    a = (_silu(g) * u).astype(BF16)
    y = x + 0.5 * _dot(a, wd_ref[...])
    if final_norm:
        y = _rmsnorm(y, fgain_ref[...])
    o_ref[...] = y
    for src, dst in zip(cast_in, cast_out):
        dst[...] = src[...].astype(BF16)


def _slab_spec(rows, cols, steps):
    n_slabs = max(k for k in range(1, steps + 1) if steps % k == 0 and rows % (16 * k) == 0)
    return pl.BlockSpec((rows // n_slabs, cols), lambda i: (i // (steps // n_slabs), 0))


def _ffn(x, gain, wg, wu, wd, *, mix=None, final_gain=None, cast=()):
    n, d = x.shape
    f = wg.shape[1]
    steps = n // FFN_ROWS
    rows = pl.BlockSpec((FFN_ROWS, d), lambda i: (i, 0))
    args, specs = [x], [rows]
    if mix is not None:
        osb, ohg, wo = mix
        args += [osb, ohg, wo]
        specs += [pl.BlockSpec((FFN_ROWS, SB_WIDTH), lambda i: (i, 0)),
                  pl.BlockSpec((FFN_ROWS, HG_WIDTH), lambda i: (i, 0)),
                  _resident(wo.shape)]
    args += [gain.reshape(1, d), wg, wu, wd]
    specs += [_resident((1, d)), _resident((d, f)), _resident((d, f)), _resident((f, d))]
    if final_gain is not None:
        args.append(final_gain.reshape(1, d))
        specs.append(_resident((1, d)))
    cast_specs = [_slab_spec(*w.shape, steps) for w in cast]
    out = pl.pallas_call(
        functools.partial(_ffn_kernel, mix_in=mix is not None, final_norm=final_gain is not None, n_cast=len(cast)),
        grid=(steps,),
        in_specs=specs + cast_specs,
        out_specs=[rows] + cast_specs,
        out_shape=[jax.ShapeDtypeStruct((n, d), F32)] + [jax.ShapeDtypeStruct(w.shape, BF16) for w in cast],
        compiler_params=pltpu.CompilerParams(dimension_semantics=("arbitrary",),
                                             vmem_limit_bytes=V7X_VMEM_LIMIT),
        name="ffn_mix" if mix is not None else "ffn",
    )(*args, *cast)
    return out[0] if not cast else out


def _sb_kernel(q_ref, k_ref, v_ref, gain_ref, tri_ref, o_ref, sp_buf, lb_buf, rs_buf, carry_ref, acc_ref):
    i = pl.program_id(2)
    half = SB_TQ // 2
    lane = lax.broadcasted_iota(jnp.int32, (1, LANES), 1)
    row = lax.broadcasted_iota(jnp.int32, (SB_TK, SB_TK), 0)
    col = lax.broadcasted_iota(jnp.int32, (SB_TK, SB_TK), 1)
    causal = col < row
    q = q_ref[0]
    n_heads = LANES // SB_HEAD_DIM
    in_head = [(lane >= h * SB_HEAD_DIM) & (lane < (h + 1) * SB_HEAD_DIM) for h in range(n_heads)]
    qh = [jnp.where(m, q, 0.0).astype(BF16) for m in in_head]

    def logits(qrows, kb, mask):
        z = _dot_nt(qrows, kb)
        log1p_term = jnp.log2(1.0 + jnp.exp2(-jnp.abs(z)))
        softplus = jnp.maximum(z + log1p_term, log1p_term)
        log_beta = z - softplus
        if mask is not None:
            softplus = jnp.where(mask, softplus, 0.0)
        return softplus, log_beta

    def weights(softplus_bf16, log_beta, carry):
        tail = _dot(softplus_bf16, tri_ref[...])
        return jnp.exp2(log_beta - tail - carry).astype(BF16)

    def rows_of(kb):
        return pl.ds(pl.multiple_of(kb * SB_TK, SB_TK), SB_TK)

    upper, lower, every = slice(0, half), slice(half, SB_TQ), slice(0, SB_TQ)
    blocks = [jnp.maximum(2 * i - 1, 0), 2 * i, 2 * i + 1]
    k_near = [k_ref[0, rows_of(kb), :] for kb in blocks]
    v_near = [v_ref[0, rows_of(kb), :] for kb in blocks]
    no_block = jnp.where(i > 0, 0.0, jnp.inf)
    near = [[(logits(qh[h][rows], k_near[r + 1], causal), logits(qh[h][rows], k_near[r], None))
             for r, rows in enumerate((upper, lower))] for h in range(n_heads)]
    carries = []
    for h in range(n_heads):
        tail = _dot(jnp.concatenate([sp for pair in near[h] for sp, _ in pair], axis=0).astype(BF16), tri_ref[...])
        for r, rows in enumerate((upper, lower)):
            (sp_d, lb_d), (sp_f, lb_f) = near[h][r]
            tail_d, tail_f = tail[2 * r * half:(2 * r + 1) * half], tail[(2 * r + 1) * half:(2 * r + 2) * half]
            carry_d = jnp.sum(sp_d, axis=1, keepdims=True)
            carries.append(carry_d + jnp.sum(sp_f, axis=1, keepdims=True))
            carry_ref[h, rows] = carries[-1]
            w_d = jnp.where(causal, jnp.exp2(lb_d - tail_d), 0.0).astype(BF16)
            w_f = jnp.exp2(lb_f - tail_f - (carry_d + no_block if r == 0 else carry_d)).astype(BF16)
            acc_ref[h, rows] = _dot(jnp.concatenate([w_d, w_f], axis=1),
                                    jnp.concatenate([v_near[r + 1], v_near[r]], axis=0))
    near_decay = jnp.min(functools.reduce(jnp.minimum, carries))

    def stage_logits(slot, kb, rows=every):
        k_blk = k_ref[0, rows_of(kb), :]
        for h in range(n_heads):
            softplus, log_beta = logits(qh[h][rows], k_blk, None)
            sp_buf[slot, h, rows] = softplus.astype(BF16)
            lb_buf[slot, h, rows] = log_beta
            rs_buf[slot, h, rows] = jnp.sum(softplus, axis=1, keepdims=True)

    def stage_values(slot, kb, rows=every):
        v_blk = v_ref[0, rows_of(kb), :]
        for h in range(n_heads):
            carry = carry_ref[h, rows]
            w = weights(sp_buf[slot, h, rows], lb_buf[slot, h, rows], carry)
            acc_ref[h, rows] += _dot(w, v_blk)
            carry_ref[h, rows] = carry + rs_buf[slot, h, rows]

    def decay():
        return jnp.min(carry_ref[...])

    top = 2 * i - 1

    def write_output():
        out = jnp.zeros((SB_TQ, LANES), F32)
        for h in range(n_heads):
            acc = acc_ref[h]
            ms = jnp.sum(jnp.where(in_head[h], acc * acc, 0.0), axis=1, keepdims=True) * (1.0 / SB_HEAD_DIM)
            out = out + jnp.where(in_head[h], acc * lax.rsqrt(ms + EPS), 0.0)
        o_ref[0] = (out * gain_ref[...]).astype(o_ref.dtype)

    write_output()

    @pl.when((i > 0) & (near_decay < SB_DEAD_LOG2))
    def _():
        @pl.when(jnp.min(carry_ref[:, lower]) < SB_DEAD_LOG2)
        def _():
            stage_logits(0, top, lower)
            stage_values(0, top, lower)

        @pl.when(decay() < SB_DEAD_LOG2)
        def _():
            stage_logits(1, top - 1)
            stage_values(1, top - 1)

        @pl.when((i > 1) & (decay() < SB_DEAD_LOG2))
        def _():
            first = top - 2
            stage_logits(0, first)

            def body(state):
                p, _ = state
                swept = decay()
                b0 = first - 2 * p
                stage_logits(1, b0 - 1)
                stage_values(0, b0)
                stage_logits(0, b0 - 2)
                stage_values(1, b0 - 1)
                return p + 1, swept

            _, reached = lax.while_loop(lambda state: (state[0] < i - 2) & (state[1] < SB_DEAD_LOG2), body,
                                        (jnp.int32(0), decay()))

            @pl.when(reached < SB_DEAD_LOG2)
            def _():
                stage_logits(1, 0)
                stage_values(0, 1)
                stage_values(1, 0)

        write_output()


def _sb_attention(qkv, gain, batch, seq):
    pairs = SB_WIDTH // LANES
    n_heads = LANES // SB_HEAD_DIM
    tri = np.tril(np.ones((SB_TK, SB_TK), np.float32), -1)
    return pl.pallas_call(
        _sb_kernel,
        grid=(batch, pairs, seq // SB_TQ),
        in_specs=[pl.BlockSpec((1, SB_TQ, LANES), lambda b, p, i: (b, i, p)),
                  pl.BlockSpec((1, seq, LANES), lambda b, p, i: (b, 0, pairs + p)),
                  pl.BlockSpec((1, seq, LANES), lambda b, p, i: (b, 0, 2 * pairs + p)),
                  pl.BlockSpec((1, LANES), lambda b, p, i: (0, p)),
                  _resident((SB_TK, SB_TK))],
        out_specs=pl.BlockSpec((1, SB_TQ, LANES), lambda b, p, i: (b, i, p)),
        out_shape=jax.ShapeDtypeStruct((batch, seq, SB_WIDTH), BF16),
        scratch_shapes=[pltpu.VMEM((2, n_heads, SB_TQ, SB_TK), BF16),
                        pltpu.VMEM((2, n_heads, SB_TQ, SB_TK), F32),
                        pltpu.VMEM((2, n_heads, SB_TQ, 1), F32),
                        pltpu.VMEM((n_heads, SB_TQ, 1), F32),
                        pltpu.VMEM((n_heads, SB_TQ, LANES), F32)],
        compiler_params=pltpu.CompilerParams(dimension_semantics=("parallel", "parallel", "parallel"),
                                             vmem_limit_bytes=V7X_VMEM_LIMIT),
        name="sb_attn",
    )(qkv, qkv, qkv, gain.reshape(1, SB_WIDTH), jnp.asarray(tri, BF16))


def _hg_levels():
    return [HG_CHUNK >> (l + 1) for l in range(HG_CHUNK.bit_length() - 1)]


def _hg_constants():
    c = HG_CHUNK
    t = np.arange(c)[:, None]
    j = np.arange(c)[None, :]
    masks = [(t == j)]
    for h in _hg_levels():
        masks.append((t // (2 * h) == j // (2 * h)) & (t % (2 * h) >= h) & (j % (2 * h) < h))
    return (j <= t).astype(np.float32), np.stack(masks).astype(np.float32)


def _hg_level_exponent(b, log_f, h):
    c = b.shape[0]
    if h >= 4:
        blocks = b.reshape(c // (2 * h), 2 * h, LANES)
        mid = jnp.broadcast_to(blocks[:, h - 1:h, :], blocks.shape).reshape(c, LANES)
        return -jnp.abs(b - mid)
    pos = lax.broadcasted_iota(jnp.int32, (c, LANES), 0) % (2 * h)
    if h == 1:
        return jnp.where(pos == 1, log_f, 0.0)
    prev = pltpu.roll(log_f, 1, 0)
    nxt = pltpu.roll(log_f, c - 1, 0)
    return jnp.where(pos == 0, nxt, jnp.where(pos == 2, log_f, jnp.where(pos == 3, log_f + prev, 0.0)))


def _hg_head(q_raw, f_raw, v_raw, g_raw, lower, gain, state, tril_ref, masks_ref, side_jobs):
    c = HG_CHUNK
    n_chunks = q_raw.shape[0] // c
    side_jobs = iter(side_jobs)

    def side_job():
        next(side_jobs, lambda: None)()

    def chunk(a, ci):
        return a[ci * c:(ci + 1) * c]

    q = _silu(q_raw)
    forget = lower + (1.0 - lower) * (1.0 / (1.0 + jnp.exp(-f_raw)))
    k = 1.0 - forget
    v = v_raw.astype(BF16)
    log_f = jnp.log(forget)
    hi = log_f.astype(BF16)
    rest = log_f - hi.astype(F32)
    mid = rest.astype(BF16)
    lo = (rest - mid.astype(F32)).astype(BF16)
    terms = jnp.concatenate([hi, mid, lo], axis=1)
    parts = jnp.concatenate([_dot(tril_ref[...], chunk(terms, ci)) for ci in range(n_chunks)], axis=0)
    b = parts[:, :LANES] + parts[:, LANES:2 * LANES] + parts[:, 2 * LANES:]
    b_last = [b[(ci + 1) * c - 1:(ci + 1) * c, :] for ci in range(n_chunks)]

    states = [state]
    for ci in range(n_chunks):
        k_out = (chunk(k, ci) * jnp.exp(b_last[ci] - chunk(b, ci))).astype(BF16)
        states.append(states[ci] * jnp.exp(b_last[ci]) + _dot_tn(chunk(v, ci), k_out))
    side_job()

    q_bf, k_bf = q.astype(BF16), k.astype(BF16)
    scores = [masks_ref[0] * _dot_nt(chunk(q_bf, ci), chunk(k_bf, ci)) for ci in range(n_chunks)]
    for l, h in enumerate(_hg_levels()):
        decay = jnp.exp(_hg_level_exponent(b, log_f, h)).astype(BF16)
        q_l, k_l = q_bf * decay, k_bf * decay
        for ci in range(n_chunks):
            scores[ci] = scores[ci] + masks_ref[1 + l] * _dot_nt(chunk(q_l, ci), chunk(k_l, ci))
        side_job()

    q_in = (q * jnp.exp(b)).astype(BF16)
    outs = [_dot(jnp.concatenate([scores[ci].astype(BF16), chunk(q_in, ci)], axis=1),
                 jnp.concatenate([chunk(v, ci), states[ci].astype(BF16).T], axis=0)) for ci in range(n_chunks)]
    o = _rmsnorm(jnp.concatenate(outs, axis=0), gain) * _silu(g_raw)
    for _ in side_jobs:
        _()
    return o.astype(BF16), states[n_chunks]


def _mix_in_kernel(x_ref, ngain_ref, w_ref, lbl_ref, hgain_ref, tril_ref, masks_ref, sb_ref, o_ref,
                   h_scr, hg_scr, state_ref):
    @pl.when(pl.program_id(1) == 0)
    def _():
        state_ref[...] = jnp.zeros_like(state_ref)

    heads = HG_WIDTH // HG_HEAD_DIM
    sb_cols = sb_ref.shape[1]
    h_scr[...] = _rmsnorm(x_ref[...], ngain_ref[...]).astype(BF16)

    def project_head(h, part):
        def job():
            cols = [sb_cols + (2 * part + j) * HG_WIDTH + h * HG_HEAD_DIM for j in range(2)]
            w = jnp.concatenate([w_ref[:, c0:c0 + HG_HEAD_DIM] for c0 in cols], axis=1)
            hg_scr[h, :, 2 * part * HG_HEAD_DIM:(2 * part + 2) * HG_HEAD_DIM] = _dot(h_scr[...], w)
        return job

    def project_sb(block, width=2 * LANES):
        def job():
            cols = slice(block * width, (block + 1) * width)
            p = _dot(h_scr[...], w_ref[:, cols])
            sb_ref[:, cols] = ((p * SB_Q_SCALE) if block * width < SB_WIDTH else p).astype(BF16)
        return job

    logits = lbl_ref[...]
    e = jnp.exp(logits - jnp.max(logits, axis=0, keepdims=True))
    lower = e[0:1, :] / jnp.sum(e, axis=0, keepdims=True)

    project_head(0, 0)()
    project_head(0, 1)()
    for h in range(heads):
        lanes = slice(h * HG_HEAD_DIM, (h + 1) * HG_HEAD_DIM)
        if h + 1 < heads:
            jobs = [lambda: None, project_head(h + 1, 0), lambda: None, lambda: None, lambda: None, project_head(h + 1, 1)]
        else:
            jobs = [project_sb(blk) for blk in range(sb_cols // (2 * LANES))]
        q, f, v, g = (hg_scr[h, :, j * HG_HEAD_DIM:(j + 1) * HG_HEAD_DIM] for j in range(4))
        o, state_ref[h] = _hg_head(q, f, v, g, lower[:, lanes], hgain_ref[:, lanes], state_ref[h],
                                   tril_ref, masks_ref, jobs)
        o_ref[:, lanes] = o


def _mix_in(x, gain, w_in, lb_logits, hg_gain, batch, seq):
    n, d = x.shape
    sb_cols = 3 * SB_WIDTH
    heads = HG_WIDTH // HG_HEAD_DIM
    tiles = seq // MIX_ROWS
    tril, masks = _hg_constants()
    rows = lambda width: pl.BlockSpec((MIX_ROWS, width), lambda b, t: (b * tiles + t, 0))
    return pl.pallas_call(
        _mix_in_kernel,
        grid=(batch, tiles),
        in_specs=[rows(d), _resident((1, d)), _resident(w_in.shape), _resident(lb_logits.shape),
                  _resident((1, HG_WIDTH)), _resident(tril.shape), _resident(masks.shape)],
        out_specs=[rows(sb_cols), rows(HG_WIDTH)],
        out_shape=[jax.ShapeDtypeStruct((n, sb_cols), BF16), jax.ShapeDtypeStruct((n, HG_WIDTH), BF16)],
        scratch_shapes=[pltpu.VMEM((MIX_ROWS, d), BF16),
                        pltpu.VMEM((heads, MIX_ROWS, 4 * HG_HEAD_DIM), F32),
                        pltpu.VMEM((heads, HG_HEAD_DIM, HG_HEAD_DIM), F32)],
        compiler_params=pltpu.CompilerParams(dimension_semantics=("parallel", "arbitrary"),
                                             vmem_limit_bytes=V7X_VMEM_LIMIT),
        name="mix_in",
    )(x, gain.reshape(1, d), w_in, lb_logits, hg_gain.reshape(1, HG_WIDTH),
      jnp.asarray(tril, BF16), jnp.asarray(masks, F32))


def kernel(x, ffn1_norm, ffn1_w_gate, ffn1_w_up, ffn1_w_down, mix_norm, w_in, sb_out_norm, hg_lower_bound_logits, hg_out_norm, w_out, ffn2_norm, ffn2_w_gate, ffn2_w_up, ffn2_w_down, final_norm):
    batch, seq, d = x.shape
    assert ffn1_norm.shape[0] == 1, "single-layer kernel"
    assert w_in.shape[2] == 3 * SB_WIDTH + 4 * HG_WIDTH and w_out.shape[1] == SB_WIDTH + HG_WIDTH
    assert seq % max(SB_TQ, MIX_ROWS) == 0 and (batch * seq) % FFN_ROWS == 0 and SB_TQ == 2 * SB_TK
    n = batch * seq
    bf = lambda w: w[0].astype(BF16)

    x1, w_in_bf, w_out_bf, wg2, wu2, wd2 = _ffn(
        x.reshape(n, d), ffn1_norm[0], bf(ffn1_w_gate), bf(ffn1_w_up), bf(ffn1_w_down),
        cast=(w_in[0], w_out[0], ffn2_w_gate[0], ffn2_w_up[0], ffn2_w_down[0]))
    sb, o_hg = _mix_in(x1, mix_norm[0], w_in_bf, hg_lower_bound_logits, hg_out_norm[0], batch, seq)
    o_sb = _sb_attention(sb.reshape(batch, seq, -1), sb_out_norm[0], batch, seq)
    out = _ffn(x1, ffn2_norm[0], wg2, wu2, wd2, mix=(o_sb.reshape(n, -1), o_hg, w_out_bf), final_gain=final_norm)
    return out.reshape(batch, seq, d)
```

```python
import functools

import jax
import jax.numpy as jnp
import numpy as np
from jax import lax
from jax.experimental import pallas as pl
from jax.experimental.pallas import tpu as pltpu

F32 = jnp.float32
BF16 = jnp.bfloat16

EPS = 1e-6
LANES = 128
SB_HEAD_DIM = 64
SB_WIDTH = 512
HG_HEAD_DIM = 128
HG_WIDTH = 512
V7X_VMEM_LIMIT = 56 * 1024 * 1024

FFN_ROWS = 512
SB_TQ = 512
SB_TK = 256
SB_LANES = 256
SB_Q_SCALE = float(np.log2(np.e)) * SB_HEAD_DIM ** -0.5
SB_DEAD_LOG2 = 160.0
HG_CHUNK = 128
MIX_ROWS = 1024


def _dot(a, b):
    return jnp.dot(a, b, preferred_element_type=F32)


def _dot_nt(a, b):
    return lax.dot_general(a, b, (((1,), (1,)), ((), ())), preferred_element_type=F32)


def _dot_tn(a, b):
    return lax.dot_general(a, b, (((0,), (0,)), ((), ())), preferred_element_type=F32)


def _rmsnorm(x, gain):
    return x * lax.rsqrt(jnp.mean(x * x, axis=-1, keepdims=True) + EPS) * gain


def _silu(x):
    return x * (1.0 / (1.0 + jnp.exp(-x)))


def _resident(shape):
    return pl.BlockSpec(shape, lambda *_: (0,) * len(shape), pipeline_mode=pl.Buffered(1))


def _ffn_kernel(*refs, mix_in, final_norm, n_cast):
    refs = list(refs)
    x_ref = refs.pop(0)
    if mix_in:
        osb_ref, ohg_ref, wo_ref = refs[:3]
        del refs[:3]
    gain_ref, wg_ref, wu_ref, wd_ref = refs[:4]
    del refs[:4]
    if final_norm:
        fgain_ref = refs.pop(0)
    cast_in, o_ref, cast_out = refs[:n_cast], refs[n_cast], refs[n_cast + 1:]
    x = x_ref[...]
    if mix_in:
        x = x + _dot(osb_ref[...], wo_ref[:SB_WIDTH, :]) + _dot(ohg_ref[...], wo_ref[SB_WIDTH:, :])
    h = _rmsnorm(x, gain_ref[...]).astype(BF16)
    g = _dot(h, wg_ref[...])
    u = _dot(h, wu_ref[...])
    a = (_silu(g) * u).astype(BF16)
    y = x + 0.5 * _dot(a, wd_ref[...])
    if final_norm:
        y = _rmsnorm(y, fgain_ref[...])
    o_ref[...] = y
    for src, dst in zip(cast_in, cast_out):
        dst[...] = src[...].astype(BF16)


def _slab_spec(rows, cols, steps):
    n_slabs = max(k for k in range(1, steps + 1) if steps % k == 0 and rows % (16 * k) == 0)
    return pl.BlockSpec((rows // n_slabs, cols), lambda i: (i // (steps // n_slabs), 0))


def _ffn(x, gain, wg, wu, wd, *, mix=None, final_gain=None, cast=()):
    n, d = x.shape
    f = wg.shape[1]
    steps = n // FFN_ROWS
    rows = pl.BlockSpec((FFN_ROWS, d), lambda i: (i, 0))
    args, specs = [x], [rows]
    if mix is not None:
        osb, ohg, wo = mix
        args += [osb, ohg, wo]
        specs += [pl.BlockSpec((FFN_ROWS, SB_WIDTH), lambda i: (i, 0)),
                  pl.BlockSpec((FFN_ROWS, HG_WIDTH), lambda i: (i, 0)),
                  _resident(wo.shape)]
    args += [gain.reshape(1, d), wg, wu, wd]
    specs += [_resident((1, d)), _resident((d, f)), _resident((d, f)), _resident((f, d))]
    if final_gain is not None:
        args.append(final_gain.reshape(1, d))
        specs.append(_resident((1, d)))
    cast_specs = [_slab_spec(*w.shape, steps) for w in cast]
    out = pl.pallas_call(
        functools.partial(_ffn_kernel, mix_in=mix is not None, final_norm=final_gain is not None, n_cast=len(cast)),
        grid=(steps,),
        in_specs=specs + cast_specs,
        out_specs=[rows] + cast_specs,
        out_shape=[jax.ShapeDtypeStruct((n, d), F32)] + [jax.ShapeDtypeStruct(w.shape, BF16) for w in cast],
        compiler_params=pltpu.CompilerParams(dimension_semantics=("arbitrary",),
                                             vmem_limit_bytes=V7X_VMEM_LIMIT),
        name="ffn_mix" if mix is not None else "ffn",
    )(*args, *cast)
    return out[0] if not cast else out


def _sb_kernel(q_ref, k_ref, v_ref, gain_ref, tri_ref, o_ref, sp_buf, lb_buf, rs_buf, carry_ref, acc_ref):
    i = pl.program_id(2)
    half = SB_TQ // 2
    lane = lax.broadcasted_iota(jnp.int32, (1, LANES), 1)
    row = lax.broadcasted_iota(jnp.int32, (SB_TK, SB_TK), 0)
    col = lax.broadcasted_iota(jnp.int32, (SB_TK, SB_TK), 1)
    causal = col < row
    n_heads = SB_LANES // SB_HEAD_DIM
    per_block = LANES // SB_HEAD_DIM
    lanes_of = [slice(h // per_block * LANES, (h // per_block + 1) * LANES) for h in range(n_heads)]
    in_head = [(lane >= (h % per_block) * SB_HEAD_DIM) & (lane < (h % per_block + 1) * SB_HEAD_DIM)
               for h in range(n_heads)]
    qh = [jnp.where(in_head[h], q_ref[0, :, lanes_of[h]], 0.0).astype(BF16) for h in range(n_heads)]

    def logits(qrows, kb, mask):
        z = _dot_nt(qrows, kb)
        log1p_term = jnp.log2(1.0 + jnp.exp2(-jnp.abs(z)))
        softplus = jnp.maximum(z + log1p_term, log1p_term)
        log_beta = z - softplus
        if mask is not None:
            softplus = jnp.where(mask, softplus, 0.0)
        return softplus, log_beta

    def weights(softplus_bf16, log_beta, carry):
        tail = _dot(softplus_bf16, tri_ref[...])
        return jnp.exp2(log_beta - tail - carry).astype(BF16)

    def rows_of(kb):
        return pl.ds(pl.multiple_of(kb * SB_TK, SB_TK), SB_TK)

    upper, lower, every = slice(0, half), slice(half, SB_TQ), slice(0, SB_TQ)
    blocks = [jnp.maximum(2 * i - 1, 0), 2 * i, 2 * i + 1]
    k_near = [k_ref[0, rows_of(kb), :] for kb in blocks]
    v_near = [v_ref[0, rows_of(kb), :] for kb in blocks]
    no_block = jnp.where(i > 0, 0.0, jnp.inf)
    near = [[(logits(qh[h][rows], k_near[r + 1][:, lanes_of[h]], causal),
              logits(qh[h][rows], k_near[r][:, lanes_of[h]], None))
             for r, rows in enumerate((upper, lower))] for h in range(n_heads)]
    carries = []
    for h in range(n_heads):
        tail = _dot(jnp.concatenate([sp for pair in near[h] for sp, _ in pair], axis=0).astype(BF16), tri_ref[...])
        for r, rows in enumerate((upper, lower)):
            (sp_d, lb_d), (sp_f, lb_f) = near[h][r]
            tail_d, tail_f = tail[2 * r * half:(2 * r + 1) * half], tail[(2 * r + 1) * half:(2 * r + 2) * half]
            carry_d = jnp.sum(sp_d, axis=1, keepdims=True)
            carries.append(carry_d + jnp.sum(sp_f, axis=1, keepdims=True))
            carry_ref[h, rows] = carries[-1]
            w_d = jnp.where(causal, jnp.exp2(lb_d - tail_d), 0.0).astype(BF16)
            w_f = jnp.exp2(lb_f - tail_f - (carry_d + no_block if r == 0 else carry_d)).astype(BF16)
            acc_ref[h, rows] = _dot(jnp.concatenate([w_d, w_f], axis=1),
                                    jnp.concatenate([v_near[r + 1][:, lanes_of[h]], v_near[r][:, lanes_of[h]]], axis=0))
    near_decay = jnp.min(functools.reduce(jnp.minimum, carries))

    def stage_logits(slot, kb, rows=every):
        k_blk = k_ref[0, rows_of(kb), :]
        for h in range(n_heads):
            softplus, log_beta = logits(qh[h][rows], k_blk[:, lanes_of[h]], None)
            sp_buf[slot, h, rows] = softplus.astype(BF16)
            lb_buf[slot, h, rows] = log_beta
            rs_buf[slot, h, rows] = jnp.sum(softplus, axis=1, keepdims=True)

    def stage_values(slot, kb, rows=every):
        v_blk = v_ref[0, rows_of(kb), :]
        for h in range(n_heads):
            carry = carry_ref[h, rows]
            w = weights(sp_buf[slot, h, rows], lb_buf[slot, h, rows], carry)
            acc_ref[h, rows] += _dot(w, v_blk[:, lanes_of[h]])
            carry_ref[h, rows] = carry + rs_buf[slot, h, rows]

    def decay():
        return jnp.min(carry_ref[...])

    top = 2 * i - 1

    def write_output():
        for blk in range(n_heads // per_block):
            out = jnp.zeros((SB_TQ, LANES), F32)
            for h in range(blk * per_block, (blk + 1) * per_block):
                acc = acc_ref[h]
                ms = jnp.sum(jnp.where(in_head[h], acc * acc, 0.0), axis=1, keepdims=True) * (1.0 / SB_HEAD_DIM)
                out = out + jnp.where(in_head[h], acc * lax.rsqrt(ms + EPS), 0.0)
            lanes = lanes_of[blk * per_block]
            o_ref[0, :, lanes] = (out * gain_ref[:, lanes]).astype(o_ref.dtype)

    write_output()

    @pl.when((i > 0) & (near_decay < SB_DEAD_LOG2))
    def _():
        @pl.when(jnp.min(carry_ref[:, lower]) < SB_DEAD_LOG2)
        def _():
            stage_logits(0, top, lower)
            stage_values(0, top, lower)

        @pl.when(decay() < SB_DEAD_LOG2)
        def _():
            stage_logits(1, top - 1)
            stage_values(1, top - 1)

        @pl.when((i > 1) & (decay() < SB_DEAD_LOG2))
        def _():
            first = top - 2
            stage_logits(0, first)

            def body(state):
                p, _ = state
                swept = decay()
                b0 = first - 2 * p
                stage_logits(1, b0 - 1)
                stage_values(0, b0)
                stage_logits(0, b0 - 2)
                stage_values(1, b0 - 1)
                return p + 1, swept

            _, reached = lax.while_loop(lambda state: (state[0] < i - 2) & (state[1] < SB_DEAD_LOG2), body,
                                        (jnp.int32(0), decay()))

            @pl.when(reached < SB_DEAD_LOG2)
            def _():
                stage_logits(1, 0)
                stage_values(0, 1)
                stage_values(1, 0)

        write_output()


def _sb_attention(qkv, gain, batch, seq):
    groups = SB_WIDTH // SB_LANES
    n_heads = SB_LANES // SB_HEAD_DIM
    tri = np.tril(np.ones((SB_TK, SB_TK), np.float32), -1)
    return pl.pallas_call(
        _sb_kernel,
        grid=(batch, groups, seq // SB_TQ),
        in_specs=[pl.BlockSpec((1, SB_TQ, SB_LANES), lambda b, p, i: (b, i, p)),
                  pl.BlockSpec((1, seq, SB_LANES), lambda b, p, i: (b, 0, groups + p)),
                  pl.BlockSpec((1, seq, SB_LANES), lambda b, p, i: (b, 0, 2 * groups + p)),
                  pl.BlockSpec((1, SB_LANES), lambda b, p, i: (0, p)),
                  _resident((SB_TK, SB_TK))],
        out_specs=pl.BlockSpec((1, SB_TQ, SB_LANES), lambda b, p, i: (b, i, p)),
        out_shape=jax.ShapeDtypeStruct((batch, seq, SB_WIDTH), BF16),
        scratch_shapes=[pltpu.VMEM((2, n_heads, SB_TQ, SB_TK), BF16),
                        pltpu.VMEM((2, n_heads, SB_TQ, SB_TK), F32),
                        pltpu.VMEM((2, n_heads, SB_TQ, 1), F32),
                        pltpu.VMEM((n_heads, SB_TQ, 1), F32),
                        pltpu.VMEM((n_heads, SB_TQ, LANES), F32)],
        compiler_params=pltpu.CompilerParams(dimension_semantics=("parallel", "parallel", "parallel"),
                                             vmem_limit_bytes=V7X_VMEM_LIMIT),
        name="sb_attn",
    )(qkv, qkv, qkv, gain.reshape(1, SB_WIDTH), jnp.asarray(tri, BF16))


def _hg_levels():
    return [HG_CHUNK >> (l + 1) for l in range(HG_CHUNK.bit_length() - 1)]


def _hg_constants():
    c = HG_CHUNK
    t = np.arange(c)[:, None]
    j = np.arange(c)[None, :]
    masks = [(t == j)]
    for h in _hg_levels():
        masks.append((t // (2 * h) == j // (2 * h)) & (t % (2 * h) >= h) & (j % (2 * h) < h))
    return (j <= t).astype(np.float32), np.stack(masks).astype(np.float32)


def _hg_level_exponent(b, log_f, h):
    c = b.shape[0]
    if h >= 4:
        blocks = b.reshape(c // (2 * h), 2 * h, LANES)
        mid = jnp.broadcast_to(blocks[:, h - 1:h, :], blocks.shape).reshape(c, LANES)
        return -jnp.abs(b - mid)
    pos = lax.broadcasted_iota(jnp.int32, (c, LANES), 0) % (2 * h)
    if h == 1:
        return jnp.where(pos == 1, log_f, 0.0)
    prev = pltpu.roll(log_f, 1, 0)
    nxt = pltpu.roll(log_f, c - 1, 0)
    return jnp.where(pos == 0, nxt, jnp.where(pos == 2, log_f, jnp.where(pos == 3, log_f + prev, 0.0)))


def _hg_head(q_raw, f_raw, v_raw, g_raw, lower, gain, state, tril_ref, masks_ref, side_jobs):
    c = HG_CHUNK
    n_chunks = q_raw.shape[0] // c
    side_jobs = iter(side_jobs)

    def side_job():
        next(side_jobs, lambda: None)()

    def chunk(a, ci):
        return a[ci * c:(ci + 1) * c]

    q = _silu(q_raw)
    forget = lower + (1.0 - lower) * (1.0 / (1.0 + jnp.exp(-f_raw)))
    k = 1.0 - forget
    v = v_raw.astype(BF16)
    log_f = jnp.log(forget)
    hi = log_f.astype(BF16)
    rest = log_f - hi.astype(F32)
    mid = rest.astype(BF16)
    lo = (rest - mid.astype(F32)).astype(BF16)
    terms = jnp.concatenate([hi, mid, lo], axis=1)
    parts = jnp.concatenate([_dot(tril_ref[...], chunk(terms, ci)) for ci in range(n_chunks)], axis=0)
    b = parts[:, :LANES] + parts[:, LANES:2 * LANES] + parts[:, 2 * LANES:]
    b_last = [b[(ci + 1) * c - 1:(ci + 1) * c, :] for ci in range(n_chunks)]

    states = [state]
    for ci in range(n_chunks):
        k_out = (chunk(k, ci) * jnp.exp(b_last[ci] - chunk(b, ci))).astype(BF16)
        states.append(states[ci] * jnp.exp(b_last[ci]) + _dot_tn(chunk(v, ci), k_out))
    side_job()

    q_bf, k_bf = q.astype(BF16), k.astype(BF16)
    scores = [masks_ref[0] * _dot_nt(chunk(q_bf, ci), chunk(k_bf, ci)) for ci in range(n_chunks)]
    for l, h in enumerate(_hg_levels()):
        decay = jnp.exp(_hg_level_exponent(b, log_f, h)).astype(BF16)
        q_l, k_l = q_bf * decay, k_bf * decay
        for ci in range(n_chunks):
            scores[ci] = scores[ci] + masks_ref[1 + l] * _dot_nt(chunk(q_l, ci), chunk(k_l, ci))
        side_job()

    q_in = (q * jnp.exp(b)).astype(BF16)
    outs = [_dot(jnp.concatenate([scores[ci].astype(BF16), chunk(q_in, ci)], axis=1),
                 jnp.concatenate([chunk(v, ci), states[ci].astype(BF16).T], axis=0)) for ci in range(n_chunks)]
    o = _rmsnorm(jnp.concatenate(outs, axis=0), gain) * _silu(g_raw)
    for _ in side_jobs:
        _()
    return o.astype(BF16), states[n_chunks]


def _mix_in_kernel(x_ref, ngain_ref, w_ref, lbl_ref, hgain_ref, tril_ref, masks_ref, sb_ref, o_ref,
                   h_scr, hg_scr, state_ref):
    @pl.when(pl.program_id(1) == 0)
    def _():
        state_ref[...] = jnp.zeros_like(state_ref)

    heads = HG_WIDTH // HG_HEAD_DIM
    sb_cols = sb_ref.shape[1]
    h_scr[...] = _rmsnorm(x_ref[...], ngain_ref[...]).astype(BF16)

    def project_head(h, part):
        def job():
            cols = [sb_cols + (2 * part + j) * HG_WIDTH + h * HG_HEAD_DIM for j in range(2)]
            w = jnp.concatenate([w_ref[:, c0:c0 + HG_HEAD_DIM] for c0 in cols], axis=1)
            hg_scr[h, :, 2 * part * HG_HEAD_DIM:(2 * part + 2) * HG_HEAD_DIM] = _dot(h_scr[...], w)
        return job

    def project_sb(block, width=2 * LANES):
        def job():
            cols = slice(block * width, (block + 1) * width)
            p = _dot(h_scr[...], w_ref[:, cols])
            sb_ref[:, cols] = ((p * SB_Q_SCALE) if block * width < SB_WIDTH else p).astype(BF16)
        return job

    logits = lbl_ref[...]
    e = jnp.exp(logits - jnp.max(logits, axis=0, keepdims=True))
    lower = e[0:1, :] / jnp.sum(e, axis=0, keepdims=True)

    project_head(0, 0)()
    project_head(0, 1)()
    for h in range(heads):
        lanes = slice(h * HG_HEAD_DIM, (h + 1) * HG_HEAD_DIM)
        if h + 1 < heads:
            jobs = [lambda: None, project_head(h + 1, 0), lambda: None, lambda: None, lambda: None, project_head(h + 1, 1)]
        else:
            jobs = [project_sb(blk) for blk in range(sb_cols // (2 * LANES))]
        q, f, v, g = (hg_scr[h, :, j * HG_HEAD_DIM:(j + 1) * HG_HEAD_DIM] for j in range(4))
        o, state_ref[h] = _hg_head(q, f, v, g, lower[:, lanes], hgain_ref[:, lanes], state_ref[h],
                                   tril_ref, masks_ref, jobs)
        o_ref[:, lanes] = o


def _mix_in(x, gain, w_in, lb_logits, hg_gain, batch, seq):
    n, d = x.shape
    sb_cols = 3 * SB_WIDTH
    heads = HG_WIDTH // HG_HEAD_DIM
    tiles = seq // MIX_ROWS
    tril, masks = _hg_constants()
    rows = lambda width: pl.BlockSpec((MIX_ROWS, width), lambda b, t: (b * tiles + t, 0))
    return pl.pallas_call(
        _mix_in_kernel,
        grid=(batch, tiles),
        in_specs=[rows(d), _resident((1, d)), _resident(w_in.shape), _resident(lb_logits.shape),
                  _resident((1, HG_WIDTH)), _resident(tril.shape), _resident(masks.shape)],
        out_specs=[rows(sb_cols), rows(HG_WIDTH)],
        out_shape=[jax.ShapeDtypeStruct((n, sb_cols), BF16), jax.ShapeDtypeStruct((n, HG_WIDTH), BF16)],
        scratch_shapes=[pltpu.VMEM((MIX_ROWS, d), BF16),
                        pltpu.VMEM((heads, MIX_ROWS, 4 * HG_HEAD_DIM), F32),
                        pltpu.VMEM((heads, HG_HEAD_DIM, HG_HEAD_DIM), F32)],
        compiler_params=pltpu.CompilerParams(dimension_semantics=("parallel", "arbitrary"),
                                             vmem_limit_bytes=V7X_VMEM_LIMIT),
        name="mix_in",
    )(x, gain.reshape(1, d), w_in, lb_logits, hg_gain.reshape(1, HG_WIDTH),
      jnp.asarray(tril, BF16), jnp.asarray(masks, F32))


def kernel(x, ffn1_norm, ffn1_w_gate, ffn1_w_up, ffn1_w_down, mix_norm, w_in, sb_out_norm, hg_lower_bound_logits, hg_out_norm, w_out, ffn2_norm, ffn2_w_gate, ffn2_w_up, ffn2_w_down, final_norm):
    batch, seq, d = x.shape
    assert ffn1_norm.shape[0] == 1, "single-layer kernel"
    assert w_in.shape[2] == 3 * SB_WIDTH + 4 * HG_WIDTH and w_out.shape[1] == SB_WIDTH + HG_WIDTH
    assert seq % max(SB_TQ, MIX_ROWS) == 0 and (batch * seq) % FFN_ROWS == 0 and SB_TQ == 2 * SB_TK
    n = batch * seq
    bf = lambda w: w[0].astype(BF16)

    x1, w_in_bf, w_out_bf, wg2, wu2, wd2 = _ffn(
        x.reshape(n, d), ffn1_norm[0], bf(ffn1_w_gate), bf(ffn1_w_up), bf(ffn1_w_down),
        cast=(w_in[0], w_out[0], ffn2_w_gate[0], ffn2_w_up[0], ffn2_w_down[0]))
    sb, o_hg = _mix_in(x1, mix_norm[0], w_in_bf, hg_lower_bound_logits, hg_out_norm[0], batch, seq)
    o_sb = _sb_attention(sb.reshape(batch, seq, -1), sb_out_norm[0], batch, seq)
    out = _ffn(x1, ffn2_norm[0], wg2, wu2, wd2, mix=(o_sb.reshape(n, -1), o_hg, w_out_bf), final_gain=final_norm)
    return out.reshape(batch, seq, d)
```

```python
import functools

import jax
import jax.numpy as jnp
import numpy as np
from jax import lax
from jax.experimental import pallas as pl
from jax.experimental.pallas import tpu as pltpu

F32 = jnp.float32
BF16 = jnp.bfloat16

EPS = 1e-6
LANES = 128
SB_HEAD_DIM = 64
SB_WIDTH = 512
HG_HEAD_DIM = 128
HG_WIDTH = 512
V7X_VMEM_LIMIT = 56 * 1024 * 1024

FFN_ROWS = 1024
SB_TQ = 512
SB_TK = 256
SB_LANES = 256
SB_Q_SCALE = float(np.log2(np.e)) * SB_HEAD_DIM ** -0.5
SB_DEAD_LOG2 = 160.0
HG_CHUNK = 128
MIX_ROWS = 1024


def _dot(a, b):
    return jnp.dot(a, b, preferred_element_type=F32)


def _dot_nt(a, b):
    return lax.dot_general(a, b, (((1,), (1,)), ((), ())), preferred_element_type=F32)


def _dot_tn(a, b):
    return lax.dot_general(a, b, (((0,), (0,)), ((), ())), preferred_element_type=F32)


def _rmsnorm(x, gain):
    return x * lax.rsqrt(jnp.mean(x * x, axis=-1, keepdims=True) + EPS) * gain


def _silu(x):
    return x * (1.0 / (1.0 + jnp.exp(-x)))


def _resident(shape):
    return pl.BlockSpec(shape, lambda *_: (0,) * len(shape), pipeline_mode=pl.Buffered(1))


def _ffn_kernel(*refs, mix_in, final_norm, n_cast):
    refs = list(refs)
    x_ref = refs.pop(0)
    if mix_in:
        osb_ref, ohg_ref, wo_ref = refs[:3]
        del refs[:3]
    gain_ref, wg_ref, wu_ref, wd_ref = refs[:4]
    del refs[:4]
    if final_norm:
        fgain_ref = refs.pop(0)
    cast_in, o_ref, cast_out = refs[:n_cast], refs[n_cast], refs[n_cast + 1:]
    x = x_ref[...]
    if mix_in:
        x = x + _dot(osb_ref[...], wo_ref[:SB_WIDTH, :]) + _dot(ohg_ref[...], wo_ref[SB_WIDTH:, :])
    h = _rmsnorm(x, gain_ref[...]).astype(BF16)
    g = _dot(h, wg_ref[...])
    u = _dot(h, wu_ref[...])
    a = (_silu(g) * u).astype(BF16)
    y = x + 0.5 * _dot(a, wd_ref[...])
    if final_norm:
        y = _rmsnorm(y, fgain_ref[...])
    o_ref[...] = y
    for src, dst in zip(cast_in, cast_out):
        dst[...] = src[...].astype(BF16)


def _slab_spec(rows, cols, steps):
    n_slabs = max(k for k in range(1, steps + 1) if steps % k == 0 and rows % (16 * k) == 0)
    return pl.BlockSpec((rows // n_slabs, cols), lambda i: (i // (steps // n_slabs), 0))


def _ffn(x, gain, wg, wu, wd, *, mix=None, final_gain=None, cast=()):
    n, d = x.shape
    f = wg.shape[1]
    steps = n // FFN_ROWS
    rows = pl.BlockSpec((FFN_ROWS, d), lambda i: (i, 0))
    args, specs = [x], [rows]
    if mix is not None:
        osb, ohg, wo = mix
        args += [osb, ohg, wo]
        specs += [pl.BlockSpec((FFN_ROWS, SB_WIDTH), lambda i: (i, 0)),
                  pl.BlockSpec((FFN_ROWS, HG_WIDTH), lambda i: (i, 0)),
                  _resident(wo.shape)]
    args += [gain.reshape(1, d), wg, wu, wd]
    specs += [_resident((1, d)), _resident((d, f)), _resident((d, f)), _resident((f, d))]
    if final_gain is not None:
        args.append(final_gain.reshape(1, d))
        specs.append(_resident((1, d)))
    cast_specs = [_slab_spec(*w.shape, steps) for w in cast]
    out = pl.pallas_call(
        functools.partial(_ffn_kernel, mix_in=mix is not None, final_norm=final_gain is not None, n_cast=len(cast)),
        grid=(steps,),
        in_specs=specs + cast_specs,
        out_specs=[rows] + cast_specs,
        out_shape=[jax.ShapeDtypeStruct((n, d), F32)] + [jax.ShapeDtypeStruct(w.shape, BF16) for w in cast],
        compiler_params=pltpu.CompilerParams(dimension_semantics=("arbitrary",),
                                             vmem_limit_bytes=V7X_VMEM_LIMIT),
        name="ffn_mix" if mix is not None else "ffn",
    )(*args, *cast)
    return out[0] if not cast else out


def _sb_kernel(q_ref, k_ref, v_ref, gain_ref, tri_ref, o_ref, sp_buf, lb_buf, rs_buf, carry_ref, acc_ref):
    i = pl.program_id(2)
    half = SB_TQ // 2
    lane = lax.broadcasted_iota(jnp.int32, (1, LANES), 1)
    row = lax.broadcasted_iota(jnp.int32, (SB_TK, SB_TK), 0)
    col = lax.broadcasted_iota(jnp.int32, (SB_TK, SB_TK), 1)
    causal = col < row
    n_heads = SB_LANES // SB_HEAD_DIM
    per_block = LANES // SB_HEAD_DIM
    lanes_of = [slice(h // per_block * LANES, (h // per_block + 1) * LANES) for h in range(n_heads)]
    in_head = [(lane >= (h % per_block) * SB_HEAD_DIM) & (lane < (h % per_block + 1) * SB_HEAD_DIM)
               for h in range(n_heads)]
    qh = [jnp.where(in_head[h], q_ref[0, :, lanes_of[h]], 0.0).astype(BF16) for h in range(n_heads)]

    def logits(qrows, kb, mask):
        z = _dot_nt(qrows, kb)
        log1p_term = jnp.log2(1.0 + jnp.exp2(-jnp.abs(z)))
        softplus = jnp.maximum(z + log1p_term, log1p_term)
        log_beta = z - softplus
        if mask is not None:
            softplus = jnp.where(mask, softplus, 0.0)
        return softplus, log_beta

    def weights(softplus_bf16, log_beta, carry):
        tail = _dot(softplus_bf16, tri_ref[...])
        return jnp.exp2(log_beta - tail - carry).astype(BF16)

    def rows_of(kb):
        return pl.ds(pl.multiple_of(kb * SB_TK, SB_TK), SB_TK)

    upper, lower, every = slice(0, half), slice(half, SB_TQ), slice(0, SB_TQ)
    blocks = [jnp.maximum(2 * i - 1, 0), 2 * i, 2 * i + 1]
    k_near = [k_ref[0, rows_of(kb), :] for kb in blocks]
    v_near = [v_ref[0, rows_of(kb), :] for kb in blocks]
    no_block = jnp.where(i > 0, 0.0, jnp.inf)
    near = [[(logits(qh[h][rows], k_near[r + 1][:, lanes_of[h]], causal),
              logits(qh[h][rows], k_near[r][:, lanes_of[h]], None))
             for r, rows in enumerate((upper, lower))] for h in range(n_heads)]
    carries = []
    for h in range(n_heads):
        tail = _dot(jnp.concatenate([sp for pair in near[h] for sp, _ in pair], axis=0).astype(BF16), tri_ref[...])
        for r, rows in enumerate((upper, lower)):
            (sp_d, lb_d), (sp_f, lb_f) = near[h][r]
            tail_d, tail_f = tail[2 * r * half:(2 * r + 1) * half], tail[(2 * r + 1) * half:(2 * r + 2) * half]
            carry_d = jnp.sum(sp_d, axis=1, keepdims=True)
            carries.append(carry_d + jnp.sum(sp_f, axis=1, keepdims=True))
            carry_ref[h, rows] = carries[-1]
            w_d = jnp.where(causal, jnp.exp2(lb_d - tail_d), 0.0).astype(BF16)
            w_f = jnp.exp2(lb_f - tail_f - (carry_d + no_block if r == 0 else carry_d)).astype(BF16)
            acc_ref[h, rows] = _dot(jnp.concatenate([w_d, w_f], axis=1),
                                    jnp.concatenate([v_near[r + 1][:, lanes_of[h]], v_near[r][:, lanes_of[h]]], axis=0))
    near_decay = jnp.min(functools.reduce(jnp.minimum, carries))

    def stage_logits(slot, kb, rows=every):
        k_blk = k_ref[0, rows_of(kb), :]
        for h in range(n_heads):
            softplus, log_beta = logits(qh[h][rows], k_blk[:, lanes_of[h]], None)
            sp_buf[slot, h, rows] = softplus.astype(BF16)
            lb_buf[slot, h, rows] = log_beta
            rs_buf[slot, h, rows] = jnp.sum(softplus, axis=1, keepdims=True)

    def stage_values(slot, kb, rows=every):
        v_blk = v_ref[0, rows_of(kb), :]
        for h in range(n_heads):
            carry = carry_ref[h, rows]
            w = weights(sp_buf[slot, h, rows], lb_buf[slot, h, rows], carry)
            acc_ref[h, rows] += _dot(w, v_blk[:, lanes_of[h]])
            carry_ref[h, rows] = carry + rs_buf[slot, h, rows]

    def decay():
        return jnp.min(carry_ref[...])

    top = 2 * i - 1

    def write_output():
        for blk in range(n_heads // per_block):
            out = jnp.zeros((SB_TQ, LANES), F32)
            for h in range(blk * per_block, (blk + 1) * per_block):
                acc = acc_ref[h]
                ms = jnp.sum(jnp.where(in_head[h], acc * acc, 0.0), axis=1, keepdims=True) * (1.0 / SB_HEAD_DIM)
                out = out + jnp.where(in_head[h], acc * lax.rsqrt(ms + EPS), 0.0)
            lanes = lanes_of[blk * per_block]
            o_ref[0, :, lanes] = (out * gain_ref[:, lanes]).astype(o_ref.dtype)

    write_output()

    @pl.when((i > 0) & (near_decay < SB_DEAD_LOG2))
    def _():
        @pl.when(jnp.min(carry_ref[:, lower]) < SB_DEAD_LOG2)
        def _():
            stage_logits(0, top, lower)
            stage_values(0, top, lower)

        @pl.when(decay() < SB_DEAD_LOG2)
        def _():
            stage_logits(1, top - 1)
            stage_values(1, top - 1)

        @pl.when((i > 1) & (decay() < SB_DEAD_LOG2))
        def _():
            first = top - 2
            stage_logits(0, first)

            def body(state):
                p, _ = state
                swept = decay()
                b0 = first - 2 * p
                stage_logits(1, b0 - 1)
                stage_values(0, b0)
                stage_logits(0, b0 - 2)
                stage_values(1, b0 - 1)
                return p + 1, swept

            _, reached = lax.while_loop(lambda state: (state[0] < i - 2) & (state[1] < SB_DEAD_LOG2), body,
                                        (jnp.int32(0), decay()))

            @pl.when(reached < SB_DEAD_LOG2)
            def _():
                stage_logits(1, 0)
                stage_values(0, 1)
                stage_values(1, 0)

        write_output()


def _sb_attention(qkv, gain, batch, seq):
    groups = SB_WIDTH // SB_LANES
    n_heads = SB_LANES // SB_HEAD_DIM
    tri = np.tril(np.ones((SB_TK, SB_TK), np.float32), -1)
    return pl.pallas_call(
        _sb_kernel,
        grid=(batch, groups, seq // SB_TQ),
        in_specs=[pl.BlockSpec((1, SB_TQ, SB_LANES), lambda b, p, i: (b, i, p)),
                  pl.BlockSpec((1, seq, SB_LANES), lambda b, p, i: (b, 0, groups + p)),
                  pl.BlockSpec((1, seq, SB_LANES), lambda b, p, i: (b, 0, 2 * groups + p)),
                  pl.BlockSpec((1, SB_LANES), lambda b, p, i: (0, p)),
                  _resident((SB_TK, SB_TK))],
        out_specs=pl.BlockSpec((1, SB_TQ, SB_LANES), lambda b, p, i: (b, i, p)),
        out_shape=jax.ShapeDtypeStruct((batch, seq, SB_WIDTH), BF16),
        scratch_shapes=[pltpu.VMEM((2, n_heads, SB_TQ, SB_TK), BF16),
                        pltpu.VMEM((2, n_heads, SB_TQ, SB_TK), F32),
                        pltpu.VMEM((2, n_heads, SB_TQ, 1), F32),
                        pltpu.VMEM((n_heads, SB_TQ, 1), F32),
                        pltpu.VMEM((n_heads, SB_TQ, LANES), F32)],
        compiler_params=pltpu.CompilerParams(dimension_semantics=("parallel", "parallel", "parallel"),
                                             vmem_limit_bytes=V7X_VMEM_LIMIT),
        name="sb_attn",
    )(qkv, qkv, qkv, gain.reshape(1, SB_WIDTH), jnp.asarray(tri, BF16))


def _hg_levels():
    return [HG_CHUNK >> (l + 1) for l in range(HG_CHUNK.bit_length() - 1)]


def _hg_constants():
    c = HG_CHUNK
    t = np.arange(c)[:, None]
    j = np.arange(c)[None, :]
    masks = [(t == j)]
    for h in _hg_levels():
        masks.append((t // (2 * h) == j // (2 * h)) & (t % (2 * h) >= h) & (j % (2 * h) < h))
    return (j <= t).astype(np.float32), np.stack(masks).astype(np.float32)


def _hg_level_exponent(b, log_f, h):
    c = b.shape[0]
    if h >= 4:
        blocks = b.reshape(c // (2 * h), 2 * h, LANES)
        mid = jnp.broadcast_to(blocks[:, h - 1:h, :], blocks.shape).reshape(c, LANES)
        return -jnp.abs(b - mid)
    pos = lax.broadcasted_iota(jnp.int32, (c, LANES), 0) % (2 * h)
    if h == 1:
        return jnp.where(pos == 1, log_f, 0.0)
    prev = pltpu.roll(log_f, 1, 0)
    nxt = pltpu.roll(log_f, c - 1, 0)
    return jnp.where(pos == 0, nxt, jnp.where(pos == 2, log_f, jnp.where(pos == 3, log_f + prev, 0.0)))


def _hg_head(q_raw, f_raw, v_raw, g_raw, lower, gain, state, tril_ref, masks_ref, side_jobs):
    c = HG_CHUNK
    n_chunks = q_raw.shape[0] // c
    side_jobs = iter(side_jobs)

    def side_job():
        next(side_jobs, lambda: None)()

    def chunk(a, ci):
        return a[ci * c:(ci + 1) * c]

    q = _silu(q_raw)
    forget = lower + (1.0 - lower) * (1.0 / (1.0 + jnp.exp(-f_raw)))
    k = 1.0 - forget
    v = v_raw.astype(BF16)
    log_f = jnp.log(forget)
    hi = log_f.astype(BF16)
    rest = log_f - hi.astype(F32)
    mid = rest.astype(BF16)
    lo = (rest - mid.astype(F32)).astype(BF16)
    terms = jnp.concatenate([hi, mid, lo], axis=1)
    parts = jnp.concatenate([_dot(tril_ref[...], chunk(terms, ci)) for ci in range(n_chunks)], axis=0)
    b = parts[:, :LANES] + parts[:, LANES:2 * LANES] + parts[:, 2 * LANES:]
    b_last = [b[(ci + 1) * c - 1:(ci + 1) * c, :] for ci in range(n_chunks)]

    states = [state]
    for ci in range(n_chunks):
        k_out = (chunk(k, ci) * jnp.exp(b_last[ci] - chunk(b, ci))).astype(BF16)
        states.append(states[ci] * jnp.exp(b_last[ci]) + _dot_tn(chunk(v, ci), k_out))
    side_job()

    q_bf, k_bf = q.astype(BF16), k.astype(BF16)
    scores = [masks_ref[0] * _dot_nt(chunk(q_bf, ci), chunk(k_bf, ci)) for ci in range(n_chunks)]
    for l, h in enumerate(_hg_levels()):
        decay = jnp.exp(_hg_level_exponent(b, log_f, h)).astype(BF16)
        q_l, k_l = q_bf * decay, k_bf * decay
        for ci in range(n_chunks):
            scores[ci] = scores[ci] + masks_ref[1 + l] * _dot_nt(chunk(q_l, ci), chunk(k_l, ci))
        side_job()

    q_in = (q * jnp.exp(b)).astype(BF16)
    outs = [_dot(jnp.concatenate([scores[ci].astype(BF16), chunk(q_in, ci)], axis=1),
                 jnp.concatenate([chunk(v, ci), states[ci].astype(BF16).T], axis=0)) for ci in range(n_chunks)]
    o = _rmsnorm(jnp.concatenate(outs, axis=0), gain) * _silu(g_raw)
    for _ in side_jobs:
        _()
    return o.astype(BF16), states[n_chunks]


def _mix_in_kernel(x_ref, ngain_ref, w_ref, lbl_ref, hgain_ref, tril_ref, masks_ref, sb_ref, o_ref,
                   h_scr, hg_scr, state_ref):
    @pl.when(pl.program_id(1) == 0)
    def _():
        state_ref[...] = jnp.zeros_like(state_ref)

    heads = HG_WIDTH // HG_HEAD_DIM
    sb_cols = sb_ref.shape[1]
    h_scr[...] = _rmsnorm(x_ref[...], ngain_ref[...]).astype(BF16)

    def project_head(h, part):
        def job():
            cols = [sb_cols + (2 * part + j) * HG_WIDTH + h * HG_HEAD_DIM for j in range(2)]
            w = jnp.concatenate([w_ref[:, c0:c0 + HG_HEAD_DIM] for c0 in cols], axis=1)
            hg_scr[h, :, 2 * part * HG_HEAD_DIM:(2 * part + 2) * HG_HEAD_DIM] = _dot(h_scr[...], w)
        return job

    def project_sb(block, width=2 * LANES):
        def job():
            cols = slice(block * width, (block + 1) * width)
            p = _dot(h_scr[...], w_ref[:, cols])
            sb_ref[:, cols] = ((p * SB_Q_SCALE) if block * width < SB_WIDTH else p).astype(BF16)
        return job

    logits = lbl_ref[...]
    e = jnp.exp(logits - jnp.max(logits, axis=0, keepdims=True))
    lower = e[0:1, :] / jnp.sum(e, axis=0, keepdims=True)

    project_head(0, 0)()
    project_head(0, 1)()
    for h in range(heads):
        lanes = slice(h * HG_HEAD_DIM, (h + 1) * HG_HEAD_DIM)
        if h + 1 < heads:
            jobs = [lambda: None, project_head(h + 1, 0), lambda: None, lambda: None, lambda: None, project_head(h + 1, 1)]
        else:
            jobs = [project_sb(blk) for blk in range(sb_cols // (2 * LANES))]
        q, f, v, g = (hg_scr[h, :, j * HG_HEAD_DIM:(j + 1) * HG_HEAD_DIM] for j in range(4))
        o, state_ref[h] = _hg_head(q, f, v, g, lower[:, lanes], hgain_ref[:, lanes], state_ref[h],
                                   tril_ref, masks_ref, jobs)
        o_ref[:, lanes] = o


def _mix_in(x, gain, w_in, lb_logits, hg_gain, batch, seq):
    n, d = x.shape
    sb_cols = 3 * SB_WIDTH
    heads = HG_WIDTH // HG_HEAD_DIM
    tiles = seq // MIX_ROWS
    tril, masks = _hg_constants()
    rows = lambda width: pl.BlockSpec((MIX_ROWS, width), lambda b, t: (b * tiles + t, 0))
    return pl.pallas_call(
        _mix_in_kernel,
        grid=(batch, tiles),
        in_specs=[rows(d), _resident((1, d)), _resident(w_in.shape), _resident(lb_logits.shape),
                  _resident((1, HG_WIDTH)), _resident(tril.shape), _resident(masks.shape)],
        out_specs=[rows(sb_cols), rows(HG_WIDTH)],
        out_shape=[jax.ShapeDtypeStruct((n, sb_cols), BF16), jax.ShapeDtypeStruct((n, HG_WIDTH), BF16)],
        scratch_shapes=[pltpu.VMEM((MIX_ROWS, d), BF16),
                        pltpu.VMEM((heads, MIX_ROWS, 4 * HG_HEAD_DIM), F32),
                        pltpu.VMEM((heads, HG_HEAD_DIM, HG_HEAD_DIM), F32)],
        compiler_params=pltpu.CompilerParams(dimension_semantics=("parallel", "arbitrary"),
                                             vmem_limit_bytes=V7X_VMEM_LIMIT),
        name="mix_in",
    )(x, gain.reshape(1, d), w_in, lb_logits, hg_gain.reshape(1, HG_WIDTH),
      jnp.asarray(tril, BF16), jnp.asarray(masks, F32))


def kernel(x, ffn1_norm, ffn1_w_gate, ffn1_w_up, ffn1_w_down, mix_norm, w_in, sb_out_norm, hg_lower_bound_logits, hg_out_norm, w_out, ffn2_norm, ffn2_w_gate, ffn2_w_up, ffn2_w_down, final_norm):
    batch, seq, d = x.shape
    assert ffn1_norm.shape[0] == 1, "single-layer kernel"
    assert w_in.shape[2] == 3 * SB_WIDTH + 4 * HG_WIDTH and w_out.shape[1] == SB_WIDTH + HG_WIDTH
    assert seq % max(SB_TQ, MIX_ROWS) == 0 and (batch * seq) % FFN_ROWS == 0 and SB_TQ == 2 * SB_TK
    n = batch * seq
    bf = lambda w: w[0].astype(BF16)

    x1, w_in_bf, w_out_bf, wg2, wu2, wd2 = _ffn(
        x.reshape(n, d), ffn1_norm[0], bf(ffn1_w_gate), bf(ffn1_w_up), bf(ffn1_w_down),
        cast=(w_in[0], w_out[0], ffn2_w_gate[0], ffn2_w_up[0], ffn2_w_down[0]))
    sb, o_hg = _mix_in(x1, mix_norm[0], w_in_bf, hg_lower_bound_logits, hg_out_norm[0], batch, seq)
    o_sb = _sb_attention(sb.reshape(batch, seq, -1), sb_out_norm[0], batch, seq)
    out = _ffn(x1, ffn2_norm[0], wg2, wu2, wd2, mix=(o_sb.reshape(n, -1), o_hg, w_out_bf), final_gain=final_norm)
    return out.reshape(batch, seq, d)
```

```python
import functools

import jax
import jax.numpy as jnp
import numpy as np
from jax import lax
from jax.experimental import pallas as pl
from jax.experimental.pallas import tpu as pltpu

F32 = jnp.float32
BF16 = jnp.bfloat16

EPS = 1e-6
LANES = 128
SB_HEAD_DIM = 64
SB_WIDTH = 512
HG_HEAD_DIM = 128
HG_WIDTH = 512
V7X_VMEM_LIMIT = 56 * 1024 * 1024

FFN_ROWS = 1024
SB_TQ = 512
SB_TK = 256
SB_LANES = 512
SB_Q_SCALE = float(np.log2(np.e)) * SB_HEAD_DIM ** -0.5
SB_DEAD_LOG2 = 160.0
HG_CHUNK = 128
MIX_ROWS = 1024


def _dot(a, b):
    return jnp.dot(a, b, preferred_element_type=F32)


def _dot_nt(a, b):
    return lax.dot_general(a, b, (((1,), (1,)), ((), ())), preferred_element_type=F32)


def _dot_tn(a, b):
    return lax.dot_general(a, b, (((0,), (0,)), ((), ())), preferred_element_type=F32)


def _rmsnorm(x, gain):
    return x * lax.rsqrt(jnp.mean(x * x, axis=-1, keepdims=True) + EPS) * gain


def _silu(x):
    return x * (1.0 / (1.0 + jnp.exp(-x)))


def _resident(shape):
    return pl.BlockSpec(shape, lambda *_: (0,) * len(shape), pipeline_mode=pl.Buffered(1))


def _ffn_kernel(*refs, mix_in, final_norm, n_cast):
    refs = list(refs)
    x_ref = refs.pop(0)
    if mix_in:
        osb_ref, ohg_ref, wo_ref = refs[:3]
        del refs[:3]
    gain_ref, wg_ref, wu_ref, wd_ref = refs[:4]
    del refs[:4]
    if final_norm:
        fgain_ref = refs.pop(0)
    cast_in, o_ref, cast_out = refs[:n_cast], refs[n_cast], refs[n_cast + 1:]
    x = x_ref[...]
    if mix_in:
        x = x + _dot(osb_ref[...], wo_ref[:SB_WIDTH, :]) + _dot(ohg_ref[...], wo_ref[SB_WIDTH:, :])
    h = _rmsnorm(x, gain_ref[...]).astype(BF16)
    g = _dot(h, wg_ref[...])
    u = _dot(h, wu_ref[...])
    a = (_silu(g) * u).astype(BF16)
    y = x + 0.5 * _dot(a, wd_ref[...])
    if final_norm:
        y = _rmsnorm(y, fgain_ref[...])
    o_ref[...] = y
    for src, dst in zip(cast_in, cast_out):
        dst[...] = src[...].astype(BF16)


def _slab_spec(rows, cols, steps):
    n_slabs = max(k for k in range(1, steps + 1) if steps % k == 0 and rows % (16 * k) == 0)
    return pl.BlockSpec((rows // n_slabs, cols), lambda i: (i // (steps // n_slabs), 0))


def _ffn(x, gain, wg, wu, wd, *, mix=None, final_gain=None, cast=()):
    n, d = x.shape
    f = wg.shape[1]
    steps = n // FFN_ROWS
    rows = pl.BlockSpec((FFN_ROWS, d), lambda i: (i, 0))
    args, specs = [x], [rows]
    if mix is not None:
        osb, ohg, wo = mix
        args += [osb, ohg, wo]
        specs += [pl.BlockSpec((FFN_ROWS, SB_WIDTH), lambda i: (i, 0)),
                  pl.BlockSpec((FFN_ROWS, HG_WIDTH), lambda i: (i, 0)),
                  _resident(wo.shape)]
    args += [gain.reshape(1, d), wg, wu, wd]
    specs += [_resident((1, d)), _resident((d, f)), _resident((d, f)), _resident((f, d))]
    if final_gain is not None:
        args.append(final_gain.reshape(1, d))
        specs.append(_resident((1, d)))
    cast_specs = [_slab_spec(*w.shape, steps) for w in cast]
    out = pl.pallas_call(
        functools.partial(_ffn_kernel, mix_in=mix is not None, final_norm=final_gain is not None, n_cast=len(cast)),
        grid=(steps,),
        in_specs=specs + cast_specs,
        out_specs=[rows] + cast_specs,
        out_shape=[jax.ShapeDtypeStruct((n, d), F32)] + [jax.ShapeDtypeStruct(w.shape, BF16) for w in cast],
        compiler_params=pltpu.CompilerParams(dimension_semantics=("arbitrary",),
                                             vmem_limit_bytes=V7X_VMEM_LIMIT),
        name="ffn_mix" if mix is not None else "ffn",
    )(*args, *cast)
    return out[0] if not cast else out


def _sb_kernel(q_ref, k_ref, v_ref, gain_ref, tri_ref, o_ref, sp_buf, lb_buf, rs_buf, carry_ref, acc_ref):
    i = pl.program_id(2)
    half = SB_TQ // 2
    lane = lax.broadcasted_iota(jnp.int32, (1, LANES), 1)
    row = lax.broadcasted_iota(jnp.int32, (SB_TK, SB_TK), 0)
    col = lax.broadcasted_iota(jnp.int32, (SB_TK, SB_TK), 1)
    causal = col < row
    n_heads = SB_LANES // SB_HEAD_DIM
    per_block = LANES // SB_HEAD_DIM
    lanes_of = [slice(h // per_block * LANES, (h // per_block + 1) * LANES) for h in range(n_heads)]
    in_head = [(lane >= (h % per_block) * SB_HEAD_DIM) & (lane < (h % per_block + 1) * SB_HEAD_DIM)
               for h in range(n_heads)]
    qh = [jnp.where(in_head[h], q_ref[0, :, lanes_of[h]], 0.0).astype(BF16) for h in range(n_heads)]

    def logits(qrows, kb, mask):
        z = _dot_nt(qrows, kb)
        log1p_term = jnp.log2(1.0 + jnp.exp2(-jnp.abs(z)))
        softplus = jnp.maximum(z + log1p_term, log1p_term)
        log_beta = z - softplus
        if mask is not None:
            softplus = jnp.where(mask, softplus, 0.0)
        return softplus, log_beta

    def weights(softplus_bf16, log_beta, carry):
        tail = _dot(softplus_bf16, tri_ref[...])
        return jnp.exp2(log_beta - tail - carry).astype(BF16)

    def rows_of(kb):
        return pl.ds(pl.multiple_of(kb * SB_TK, SB_TK), SB_TK)

    upper, lower, every = slice(0, half), slice(half, SB_TQ), slice(0, SB_TQ)
    blocks = [jnp.maximum(2 * i - 1, 0), 2 * i, 2 * i + 1]
    k_near = [k_ref[0, rows_of(kb), :] for kb in blocks]
    v_near = [v_ref[0, rows_of(kb), :] for kb in blocks]
    no_block = jnp.where(i > 0, 0.0, jnp.inf)
    near = [[(logits(qh[h][rows], k_near[r + 1][:, lanes_of[h]], causal),
              logits(qh[h][rows], k_near[r][:, lanes_of[h]], None))
             for r, rows in enumerate((upper, lower))] for h in range(n_heads)]
    carries = []
    for h in range(n_heads):
        tail = _dot(jnp.concatenate([sp for pair in near[h] for sp, _ in pair], axis=0).astype(BF16), tri_ref[...])
        for r, rows in enumerate((upper, lower)):
            (sp_d, lb_d), (sp_f, lb_f) = near[h][r]
            tail_d, tail_f = tail[2 * r * half:(2 * r + 1) * half], tail[(2 * r + 1) * half:(2 * r + 2) * half]
            carry_d = jnp.sum(sp_d, axis=1, keepdims=True)
            carries.append(carry_d + jnp.sum(sp_f, axis=1, keepdims=True))
            carry_ref[h, rows] = carries[-1]
            w_d = jnp.where(causal, jnp.exp2(lb_d - tail_d), 0.0).astype(BF16)
            w_f = jnp.exp2(lb_f - tail_f - (carry_d + no_block if r == 0 else carry_d)).astype(BF16)
            acc_ref[h, rows] = _dot(jnp.concatenate([w_d, w_f], axis=1),
                                    jnp.concatenate([v_near[r + 1][:, lanes_of[h]], v_near[r][:, lanes_of[h]]], axis=0))
    near_decay = jnp.min(functools.reduce(jnp.minimum, carries))

    def stage_logits(slot, kb, rows=every):
        k_blk = k_ref[0, rows_of(kb), :]
        for h in range(n_heads):
            softplus, log_beta = logits(qh[h][rows], k_blk[:, lanes_of[h]], None)
            sp_buf[slot, h, rows] = softplus.astype(BF16)
            lb_buf[slot, h, rows] = log_beta
            rs_buf[slot, h, rows] = jnp.sum(softplus, axis=1, keepdims=True)

    def stage_values(slot, kb, rows=every):
        v_blk = v_ref[0, rows_of(kb), :]
        for h in range(n_heads):
            carry = carry_ref[h, rows]
            w = weights(sp_buf[slot, h, rows], lb_buf[slot, h, rows], carry)
            acc_ref[h, rows] += _dot(w, v_blk[:, lanes_of[h]])
            carry_ref[h, rows] = carry + rs_buf[slot, h, rows]

    def decay():
        return jnp.min(carry_ref[...])

    top = 2 * i - 1

    def write_output():
        for blk in range(n_heads // per_block):
            out = jnp.zeros((SB_TQ, LANES), F32)
            for h in range(blk * per_block, (blk + 1) * per_block):
                acc = acc_ref[h]
                ms = jnp.sum(jnp.where(in_head[h], acc * acc, 0.0), axis=1, keepdims=True) * (1.0 / SB_HEAD_DIM)
                out = out + jnp.where(in_head[h], acc * lax.rsqrt(ms + EPS), 0.0)
            lanes = lanes_of[blk * per_block]
            o_ref[0, :, lanes] = (out * gain_ref[:, lanes]).astype(o_ref.dtype)

    write_output()

    @pl.when((i > 0) & (near_decay < SB_DEAD_LOG2))
    def _():
        @pl.when(jnp.min(carry_ref[:, lower]) < SB_DEAD_LOG2)
        def _():
            stage_logits(0, top, lower)
            stage_values(0, top, lower)

        @pl.when(decay() < SB_DEAD_LOG2)
        def _():
            stage_logits(1, top - 1)
            stage_values(1, top - 1)

        @pl.when((i > 1) & (decay() < SB_DEAD_LOG2))
        def _():
            first = top - 2
            stage_logits(0, first)

            def body(state):
                p, _ = state
                swept = decay()
                b0 = first - 2 * p
                stage_logits(1, b0 - 1)
                stage_values(0, b0)
                stage_logits(0, b0 - 2)
                stage_values(1, b0 - 1)
                return p + 1, swept

            _, reached = lax.while_loop(lambda state: (state[0] < i - 2) & (state[1] < SB_DEAD_LOG2), body,
                                        (jnp.int32(0), decay()))

            @pl.when(reached < SB_DEAD_LOG2)
            def _():
                stage_logits(1, 0)
                stage_values(0, 1)
                stage_values(1, 0)

        write_output()


def _sb_attention(qkv, gain, batch, seq):
    groups = SB_WIDTH // SB_LANES
    n_heads = SB_LANES // SB_HEAD_DIM
    tri = np.tril(np.ones((SB_TK, SB_TK), np.float32), -1)
    return pl.pallas_call(
        _sb_kernel,
        grid=(batch, groups, seq // SB_TQ),
        in_specs=[pl.BlockSpec((1, SB_TQ, SB_LANES), lambda b, p, i: (b, i, p)),
                  pl.BlockSpec((1, seq, SB_LANES), lambda b, p, i: (b, 0, groups + p), pipeline_mode=pl.Buffered(1)),
                  pl.BlockSpec((1, seq, SB_LANES), lambda b, p, i: (b, 0, 2 * groups + p), pipeline_mode=pl.Buffered(1)),
                  pl.BlockSpec((1, SB_LANES), lambda b, p, i: (0, p)),
                  _resident((SB_TK, SB_TK))],
        out_specs=pl.BlockSpec((1, SB_TQ, SB_LANES), lambda b, p, i: (b, i, p)),
        out_shape=jax.ShapeDtypeStruct((batch, seq, SB_WIDTH), BF16),
        scratch_shapes=[pltpu.VMEM((2, n_heads, SB_TQ, SB_TK), BF16),
                        pltpu.VMEM((2, n_heads, SB_TQ, SB_TK), F32),
                        pltpu.VMEM((2, n_heads, SB_TQ, 1), F32),
                        pltpu.VMEM((n_heads, SB_TQ, 1), F32),
                        pltpu.VMEM((n_heads, SB_TQ, LANES), F32)],
        compiler_params=pltpu.CompilerParams(dimension_semantics=("parallel", "parallel", "parallel"),
                                             vmem_limit_bytes=V7X_VMEM_LIMIT),
        name="sb_attn",
    )(qkv, qkv, qkv, gain.reshape(1, SB_WIDTH), jnp.asarray(tri, BF16))


def _hg_levels():
    return [HG_CHUNK >> (l + 1) for l in range(HG_CHUNK.bit_length() - 1)]


def _hg_constants():
    c = HG_CHUNK
    t = np.arange(c)[:, None]
    j = np.arange(c)[None, :]
    masks = [(t == j)]
    for h in _hg_levels():
        masks.append((t // (2 * h) == j // (2 * h)) & (t % (2 * h) >= h) & (j % (2 * h) < h))
    return (j <= t).astype(np.float32), np.stack(masks).astype(np.float32)


def _hg_level_exponent(b, log_f, h):
    c = b.shape[0]
    if h >= 4:
        blocks = b.reshape(c // (2 * h), 2 * h, LANES)
        mid = jnp.broadcast_to(blocks[:, h - 1:h, :], blocks.shape).reshape(c, LANES)
        return -jnp.abs(b - mid)
    pos = lax.broadcasted_iota(jnp.int32, (c, LANES), 0) % (2 * h)
    if h == 1:
        return jnp.where(pos == 1, log_f, 0.0)
    prev = pltpu.roll(log_f, 1, 0)
    nxt = pltpu.roll(log_f, c - 1, 0)
    return jnp.where(pos == 0, nxt, jnp.where(pos == 2, log_f, jnp.where(pos == 3, log_f + prev, 0.0)))


def _hg_head(q_raw, f_raw, v_raw, g_raw, lower, gain, state, tril_ref, masks_ref, side_jobs):
    c = HG_CHUNK
    n_chunks = q_raw.shape[0] // c
    side_jobs = iter(side_jobs)

    def side_job():
        next(side_jobs, lambda: None)()

    def chunk(a, ci):
        return a[ci * c:(ci + 1) * c]

    q = _silu(q_raw)
    forget = lower + (1.0 - lower) * (1.0 / (1.0 + jnp.exp(-f_raw)))
    k = 1.0 - forget
    v = v_raw.astype(BF16)
    log_f = jnp.log(forget)
    hi = log_f.astype(BF16)
    rest = log_f - hi.astype(F32)
    mid = rest.astype(BF16)
    lo = (rest - mid.astype(F32)).astype(BF16)
    terms = jnp.concatenate([hi, mid, lo], axis=1)
    parts = jnp.concatenate([_dot(tril_ref[...], chunk(terms, ci)) for ci in range(n_chunks)], axis=0)
    b = parts[:, :LANES] + parts[:, LANES:2 * LANES] + parts[:, 2 * LANES:]
    b_last = [b[(ci + 1) * c - 1:(ci + 1) * c, :] for ci in range(n_chunks)]

    states = [state]
    for ci in range(n_chunks):
        k_out = (chunk(k, ci) * jnp.exp(b_last[ci] - chunk(b, ci))).astype(BF16)
        states.append(states[ci] * jnp.exp(b_last[ci]) + _dot_tn(chunk(v, ci), k_out))
    side_job()

    q_bf, k_bf = q.astype(BF16), k.astype(BF16)
    scores = [masks_ref[0] * _dot_nt(chunk(q_bf, ci), chunk(k_bf, ci)) for ci in range(n_chunks)]
    for l, h in enumerate(_hg_levels()):
        decay = jnp.exp(_hg_level_exponent(b, log_f, h)).astype(BF16)
        q_l, k_l = q_bf * decay, k_bf * decay
        for ci in range(n_chunks):
            scores[ci] = scores[ci] + masks_ref[1 + l] * _dot_nt(chunk(q_l, ci), chunk(k_l, ci))
        side_job()

    q_in = (q * jnp.exp(b)).astype(BF16)
    outs = [_dot(jnp.concatenate([scores[ci].astype(BF16), chunk(q_in, ci)], axis=1),
                 jnp.concatenate([chunk(v, ci), states[ci].astype(BF16).T], axis=0)) for ci in range(n_chunks)]
    o = _rmsnorm(jnp.concatenate(outs, axis=0), gain) * _silu(g_raw)
    for _ in side_jobs:
        _()
    return o.astype(BF16), states[n_chunks]


def _mix_in_kernel(x_ref, ngain_ref, w_ref, lbl_ref, hgain_ref, tril_ref, masks_ref, sb_ref, o_ref,
                   h_scr, hg_scr, state_ref):
    @pl.when(pl.program_id(1) == 0)
    def _():
        state_ref[...] = jnp.zeros_like(state_ref)

    heads = HG_WIDTH // HG_HEAD_DIM
    sb_cols = sb_ref.shape[1]
    h_scr[...] = _rmsnorm(x_ref[...], ngain_ref[...]).astype(BF16)

    def project_head(h, part):
        def job():
            cols = [sb_cols + (2 * part + j) * HG_WIDTH + h * HG_HEAD_DIM for j in range(2)]
            w = jnp.concatenate([w_ref[:, c0:c0 + HG_HEAD_DIM] for c0 in cols], axis=1)
            hg_scr[h, :, 2 * part * HG_HEAD_DIM:(2 * part + 2) * HG_HEAD_DIM] = _dot(h_scr[...], w)
        return job

    def project_sb(block, width=2 * LANES):
        def job():
            cols = slice(block * width, (block + 1) * width)
            p = _dot(h_scr[...], w_ref[:, cols])
            sb_ref[:, cols] = ((p * SB_Q_SCALE) if block * width < SB_WIDTH else p).astype(BF16)
        return job

    logits = lbl_ref[...]
    e = jnp.exp(logits - jnp.max(logits, axis=0, keepdims=True))
    lower = e[0:1, :] / jnp.sum(e, axis=0, keepdims=True)

    project_head(0, 0)()
    project_head(0, 1)()
    for h in range(heads):
        lanes = slice(h * HG_HEAD_DIM, (h + 1) * HG_HEAD_DIM)
        if h + 1 < heads:
            jobs = [lambda: None, project_head(h + 1, 0), lambda: None, lambda: None, lambda: None, project_head(h + 1, 1)]
        else:
            jobs = [project_sb(blk) for blk in range(sb_cols // (2 * LANES))]
        q, f, v, g = (hg_scr[h, :, j * HG_HEAD_DIM:(j + 1) * HG_HEAD_DIM] for j in range(4))
        o, state_ref[h] = _hg_head(q, f, v, g, lower[:, lanes], hgain_ref[:, lanes], state_ref[h],
                                   tril_ref, masks_ref, jobs)
        o_ref[:, lanes] = o


def _mix_in(x, gain, w_in, lb_logits, hg_gain, batch, seq):
    n, d = x.shape
    sb_cols = 3 * SB_WIDTH
    heads = HG_WIDTH // HG_HEAD_DIM
    tiles = seq // MIX_ROWS
    tril, masks = _hg_constants()
    rows = lambda width: pl.BlockSpec((MIX_ROWS, width), lambda b, t: (b * tiles + t, 0))
    return pl.pallas_call(
        _mix_in_kernel,
        grid=(batch, tiles),
        in_specs=[rows(d), _resident((1, d)), _resident(w_in.shape), _resident(lb_logits.shape),
                  _resident((1, HG_WIDTH)), _resident(tril.shape), _resident(masks.shape)],
        out_specs=[rows(sb_cols), rows(HG_WIDTH)],
        out_shape=[jax.ShapeDtypeStruct((n, sb_cols), BF16), jax.ShapeDtypeStruct((n, HG_WIDTH), BF16)],
        scratch_shapes=[pltpu.VMEM((MIX_ROWS, d), BF16),
                        pltpu.VMEM((heads, MIX_ROWS, 4 * HG_HEAD_DIM), F32),
                        pltpu.VMEM((heads, HG_HEAD_DIM, HG_HEAD_DIM), F32)],
        compiler_params=pltpu.CompilerParams(dimension_semantics=("parallel", "arbitrary"),
                                             vmem_limit_bytes=V7X_VMEM_LIMIT),
        name="mix_in",
    )(x, gain.reshape(1, d), w_in, lb_logits, hg_gain.reshape(1, HG_WIDTH),
      jnp.asarray(tril, BF16), jnp.asarray(masks, F32))


def kernel(x, ffn1_norm, ffn1_w_gate, ffn1_w_up, ffn1_w_down, mix_norm, w_in, sb_out_norm, hg_lower_bound_logits, hg_out_norm, w_out, ffn2_norm, ffn2_w_gate, ffn2_w_up, ffn2_w_down, final_norm):
    batch, seq, d = x.shape
    assert ffn1_norm.shape[0] == 1, "single-layer kernel"
    assert w_in.shape[2] == 3 * SB_WIDTH + 4 * HG_WIDTH and w_out.shape[1] == SB_WIDTH + HG_WIDTH
    assert seq % max(SB_TQ, MIX_ROWS) == 0 and (batch * seq) % FFN_ROWS == 0 and SB_TQ == 2 * SB_TK
    n = batch * seq
    bf = lambda w: w[0].astype(BF16)

    x1, w_in_bf, w_out_bf, wg2, wu2, wd2 = _ffn(
        x.reshape(n, d), ffn1_norm[0], bf(ffn1_w_gate), bf(ffn1_w_up), bf(ffn1_w_down),
        cast=(w_in[0], w_out[0], ffn2_w_gate[0], ffn2_w_up[0], ffn2_w_down[0]))
    sb, o_hg = _mix_in(x1, mix_norm[0], w_in_bf, hg_lower_bound_logits, hg_out_norm[0], batch, seq)
    o_sb = _sb_attention(sb.reshape(batch, seq, -1), sb_out_norm[0], batch, seq)
    out = _ffn(x1, ffn2_norm[0], wg2, wu2, wd2, mix=(o_sb.reshape(n, -1), o_hg, w_out_bf), final_gain=final_norm)
    return out.reshape(batch, seq, d)
```

```python
import functools

import jax
import jax.numpy as jnp
import numpy as np
from jax import lax
from jax.experimental import pallas as pl
from jax.experimental.pallas import tpu as pltpu

F32 = jnp.float32
BF16 = jnp.bfloat16

EPS = 1e-6
LANES = 128
SB_HEAD_DIM = 64
SB_WIDTH = 512
HG_HEAD_DIM = 128
HG_WIDTH = 512
V7X_VMEM_LIMIT = 56 * 1024 * 1024

FFN_ROWS = 1024
SB_TQ = 512
SB_TK = 256
SB_LANES = 256
SB_Q_SCALE = float(np.log2(np.e)) * SB_HEAD_DIM ** -0.5
SB_DEAD_LOG2 = 160.0
HG_CHUNK = 128
MIX_ROWS = 1024


def _dot(a, b):
    return jnp.dot(a, b, preferred_element_type=F32)


def _dot_nt(a, b):
    return lax.dot_general(a, b, (((1,), (1,)), ((), ())), preferred_element_type=F32)


def _dot_tn(a, b):
    return lax.dot_general(a, b, (((0,), (0,)), ((), ())), preferred_element_type=F32)


def _rmsnorm(x, gain):
    return x * lax.rsqrt(jnp.mean(x * x, axis=-1, keepdims=True) + EPS) * gain


def _silu(x):
    return x * (1.0 / (1.0 + jnp.exp(-x)))


def _resident(shape):
    return pl.BlockSpec(shape, lambda *_: (0,) * len(shape), pipeline_mode=pl.Buffered(1))


def _ffn_kernel(*refs, mix_in, final_norm, n_cast):
    refs = list(refs)
    x_ref = refs.pop(0)
    if mix_in:
        osb_ref, ohg_ref, wo_ref = refs[:3]
        del refs[:3]
    gain_ref, wg_ref, wu_ref, wd_ref = refs[:4]
    del refs[:4]
    if final_norm:
        fgain_ref = refs.pop(0)
    cast_in, o_ref, cast_out = refs[:n_cast], refs[n_cast], refs[n_cast + 1:]
    x = x_ref[...]
    if mix_in:
        x = x + _dot(osb_ref[...], wo_ref[:SB_WIDTH, :]) + _dot(ohg_ref[...], wo_ref[SB_WIDTH:, :])
    h = _rmsnorm(x, gain_ref[...]).astype(BF16)
    g = _dot(h, wg_ref[...])
    u = _dot(h, wu_ref[...])
    a = (_silu(g) * u).astype(BF16)
    y = x + 0.5 * _dot(a, wd_ref[...])
    if final_norm:
        y = _rmsnorm(y, fgain_ref[...])
    o_ref[...] = y
    for src, dst in zip(cast_in, cast_out):
        dst[...] = src[...].astype(BF16)


def _slab_spec(rows, cols, steps):
    n_slabs = max(k for k in range(1, steps + 1) if steps % k == 0 and rows % (16 * k) == 0)
    return pl.BlockSpec((rows // n_slabs, cols), lambda i: (i // (steps // n_slabs), 0))


def _ffn(x, gain, wg, wu, wd, *, mix=None, final_gain=None, cast=()):
    n, d = x.shape
    f = wg.shape[1]
    steps = n // FFN_ROWS
    rows = pl.BlockSpec((FFN_ROWS, d), lambda i: (i, 0))
    args, specs = [x], [rows]
    if mix is not None:
        osb, ohg, wo = mix
        args += [osb, ohg, wo]
        specs += [pl.BlockSpec((FFN_ROWS, SB_WIDTH), lambda i: (i, 0)),
                  pl.BlockSpec((FFN_ROWS, HG_WIDTH), lambda i: (i, 0)),
                  _resident(wo.shape)]
    args += [gain.reshape(1, d), wg, wu, wd]
    specs += [_resident((1, d)), _resident((d, f)), _resident((d, f)), _resident((f, d))]
    if final_gain is not None:
        args.append(final_gain.reshape(1, d))
        specs.append(_resident((1, d)))
    cast_specs = [_slab_spec(*w.shape, steps) for w in cast]
    out = pl.pallas_call(
        functools.partial(_ffn_kernel, mix_in=mix is not None, final_norm=final_gain is not None, n_cast=len(cast)),
        grid=(steps,),
        in_specs=specs + cast_specs,
        out_specs=[rows] + cast_specs,
        out_shape=[jax.ShapeDtypeStruct((n, d), F32)] + [jax.ShapeDtypeStruct(w.shape, BF16) for w in cast],
        compiler_params=pltpu.CompilerParams(dimension_semantics=("arbitrary",),
                                             vmem_limit_bytes=V7X_VMEM_LIMIT),
        name="ffn_mix" if mix is not None else "ffn",
    )(*args, *cast)
    return out[0] if not cast else out


def _sb_kernel(q_ref, k_ref, v_ref, gain_ref, tri_ref, o_ref, sp_buf, lb_buf, rs_buf, carry_ref, acc_ref):
    i = pl.program_id(2)
    half = SB_TQ // 2
    lane = lax.broadcasted_iota(jnp.int32, (1, LANES), 1)
    row = lax.broadcasted_iota(jnp.int32, (SB_TK, SB_TK), 0)
    col = lax.broadcasted_iota(jnp.int32, (SB_TK, SB_TK), 1)
    causal = col < row
    n_heads = SB_LANES // SB_HEAD_DIM
    per_block = LANES // SB_HEAD_DIM
    lanes_of = [slice(h // per_block * LANES, (h // per_block + 1) * LANES) for h in range(n_heads)]
    in_head = [(lane >= (h % per_block) * SB_HEAD_DIM) & (lane < (h % per_block + 1) * SB_HEAD_DIM)
               for h in range(n_heads)]
    qh = [jnp.where(in_head[h], q_ref[0, :, lanes_of[h]], 0.0).astype(BF16) for h in range(n_heads)]

    def logits(qrows, kb, mask):
        z = _dot_nt(qrows, kb)
        log1p_term = jnp.log2(1.0 + jnp.exp2(-jnp.abs(z)))
        softplus = jnp.maximum(z + log1p_term, log1p_term)
        log_beta = z - softplus
        if mask is not None:
            softplus = jnp.where(mask, softplus, 0.0)
        return softplus, log_beta

    def weights(softplus_bf16, log_beta, carry):
        tail = _dot(softplus_bf16, tri_ref[:, :SB_TK])
        return jnp.exp2(log_beta - tail - carry).astype(BF16)

    def rows_of(kb):
        return pl.ds(pl.multiple_of(kb * SB_TK, SB_TK), SB_TK)

    upper, lower, every = slice(0, half), slice(half, SB_TQ), slice(0, SB_TQ)
    blocks = [jnp.maximum(2 * i - 1, 0), 2 * i, 2 * i + 1]
    k_near = [k_ref[0, rows_of(kb), :] for kb in blocks]
    v_near = [v_ref[0, rows_of(kb), :] for kb in blocks]
    no_block = jnp.where(i > 0, 0.0, jnp.inf)
    near = [[(logits(qh[h][rows], k_near[r + 1][:, lanes_of[h]], causal),
              logits(qh[h][rows], k_near[r][:, lanes_of[h]], None))
             for r, rows in enumerate((upper, lower))] for h in range(n_heads)]
    carries = []
    for h in range(n_heads):
        sums = _dot(jnp.concatenate([sp for pair in near[h] for sp, _ in pair], axis=0).astype(BF16), tri_ref[...])
        tail, total = sums[:, :SB_TK], sums[:, SB_TK:SB_TK + 1]
        for r, rows in enumerate((upper, lower)):
            (sp_d, lb_d), (sp_f, lb_f) = near[h][r]
            tail_d, tail_f = tail[2 * r * half:(2 * r + 1) * half], tail[(2 * r + 1) * half:(2 * r + 2) * half]
            carry_d = total[2 * r * half:(2 * r + 1) * half]
            carries.append(carry_d + total[(2 * r + 1) * half:(2 * r + 2) * half])
            carry_ref[h, rows] = carries[-1]
            w_d = jnp.where(causal, jnp.exp2(lb_d - tail_d), 0.0).astype(BF16)
            w_f = jnp.exp2(lb_f - tail_f - (carry_d + no_block if r == 0 else carry_d)).astype(BF16)
            acc_ref[h, rows] = _dot(jnp.concatenate([w_d, w_f], axis=1),
                                    jnp.concatenate([v_near[r + 1][:, lanes_of[h]], v_near[r][:, lanes_of[h]]], axis=0))
    near_decay = jnp.min(functools.reduce(jnp.minimum, carries))

    def stage_logits(slot, kb, rows=every):
        k_blk = k_ref[0, rows_of(kb), :]
        for h in range(n_heads):
            softplus, log_beta = logits(qh[h][rows], k_blk[:, lanes_of[h]], None)
            softplus = softplus.astype(BF16)
            sp_buf[slot, h, rows] = softplus
            lb_buf[slot, h, rows] = log_beta
            rs_buf[slot, h, rows] = _dot(softplus, tri_ref[:, SB_TK:])[:, :1]

    def stage_values(slot, kb, rows=every):
        v_blk = v_ref[0, rows_of(kb), :]
        for h in range(n_heads):
            carry = carry_ref[h, rows]
            w = weights(sp_buf[slot, h, rows], lb_buf[slot, h, rows], carry)
            acc_ref[h, rows] += _dot(w, v_blk[:, lanes_of[h]])
            carry_ref[h, rows] = carry + rs_buf[slot, h, rows]

    def decay():
        return jnp.min(carry_ref[...])

    top = 2 * i - 1

    def write_output():
        for blk in range(n_heads // per_block):
            out = jnp.zeros((SB_TQ, LANES), F32)
            for h in range(blk * per_block, (blk + 1) * per_block):
                acc = acc_ref[h]
                ms = jnp.sum(jnp.where(in_head[h], acc * acc, 0.0), axis=1, keepdims=True) * (1.0 / SB_HEAD_DIM)
                out = out + jnp.where(in_head[h], acc * lax.rsqrt(ms + EPS), 0.0)
            lanes = lanes_of[blk * per_block]
            o_ref[0, :, lanes] = (out * gain_ref[:, lanes]).astype(o_ref.dtype)

    write_output()

    @pl.when((i > 0) & (near_decay < SB_DEAD_LOG2))
    def _():
        @pl.when(jnp.min(carry_ref[:, lower]) < SB_DEAD_LOG2)
        def _():
            stage_logits(0, top, lower)
            stage_values(0, top, lower)

        @pl.when(decay() < SB_DEAD_LOG2)
        def _():
            stage_logits(1, top - 1)
            stage_values(1, top - 1)

        @pl.when((i > 1) & (decay() < SB_DEAD_LOG2))
        def _():
            first = top - 2
            stage_logits(0, first)

            def body(state):
                p, _ = state
                swept = decay()
                b0 = first - 2 * p
                stage_logits(1, b0 - 1)
                stage_values(0, b0)
                stage_logits(0, b0 - 2)
                stage_values(1, b0 - 1)
                return p + 1, swept

            _, reached = lax.while_loop(lambda state: (state[0] < i - 2) & (state[1] < SB_DEAD_LOG2), body,
                                        (jnp.int32(0), decay()))

            @pl.when(reached < SB_DEAD_LOG2)
            def _():
                stage_logits(1, 0)
                stage_values(0, 1)
                stage_values(1, 0)

        write_output()


def _sb_attention(qkv, gain, batch, seq):
    groups = SB_WIDTH // SB_LANES
    n_heads = SB_LANES // SB_HEAD_DIM
    tri = np.concatenate([np.tril(np.ones((SB_TK, SB_TK), np.float32), -1), np.ones((SB_TK, LANES), np.float32)], axis=1)
    return pl.pallas_call(
        _sb_kernel,
        grid=(batch, groups, seq // SB_TQ),
        in_specs=[pl.BlockSpec((1, SB_TQ, SB_LANES), lambda b, p, i: (b, i, p)),
                  pl.BlockSpec((1, seq, SB_LANES), lambda b, p, i: (b, 0, groups + p)),
                  pl.BlockSpec((1, seq, SB_LANES), lambda b, p, i: (b, 0, 2 * groups + p)),
                  pl.BlockSpec((1, SB_LANES), lambda b, p, i: (0, p)),
                  _resident(tri.shape)],
        out_specs=pl.BlockSpec((1, SB_TQ, SB_LANES), lambda b, p, i: (b, i, p)),
        out_shape=jax.ShapeDtypeStruct((batch, seq, SB_WIDTH), BF16),
        scratch_shapes=[pltpu.VMEM((2, n_heads, SB_TQ, SB_TK), BF16),
                        pltpu.VMEM((2, n_heads, SB_TQ, SB_TK), F32),
                        pltpu.VMEM((2, n_heads, SB_TQ, 1), F32),
                        pltpu.VMEM((n_heads, SB_TQ, 1), F32),
                        pltpu.VMEM((n_heads, SB_TQ, LANES), F32)],
        compiler_params=pltpu.CompilerParams(dimension_semantics=("parallel", "parallel", "parallel"),
                                             vmem_limit_bytes=V7X_VMEM_LIMIT),
        name="sb_attn",
    )(qkv, qkv, qkv, gain.reshape(1, SB_WIDTH), jnp.asarray(tri, BF16))


def _hg_levels():
    return [HG_CHUNK >> (l + 1) for l in range(HG_CHUNK.bit_length() - 1)]


def _hg_constants():
    c = HG_CHUNK
    t = np.arange(c)[:, None]
    j = np.arange(c)[None, :]
    masks = [(t == j)]
    for h in _hg_levels():
        masks.append((t // (2 * h) == j // (2 * h)) & (t % (2 * h) >= h) & (j % (2 * h) < h))
    return (j <= t).astype(np.float32), np.stack(masks).astype(np.float32)


def _hg_level_exponent(b, log_f, h):
    c = b.shape[0]
    if h >= 4:
        blocks = b.reshape(c // (2 * h), 2 * h, LANES)
        mid = jnp.broadcast_to(blocks[:, h - 1:h, :], blocks.shape).reshape(c, LANES)
        return -jnp.abs(b - mid)
    pos = lax.broadcasted_iota(jnp.int32, (c, LANES), 0) % (2 * h)
    if h == 1:
        return jnp.where(pos == 1, log_f, 0.0)
    prev = pltpu.roll(log_f, 1, 0)
    nxt = pltpu.roll(log_f, c - 1, 0)
    return jnp.where(pos == 0, nxt, jnp.where(pos == 2, log_f, jnp.where(pos == 3, log_f + prev, 0.0)))


def _hg_head(q_raw, f_raw, v_raw, g_raw, lower, gain, state, tril_ref, masks_ref, side_jobs):
    c = HG_CHUNK
    n_chunks = q_raw.shape[0] // c
    side_jobs = iter(side_jobs)

    def side_job():
        next(side_jobs, lambda: None)()

    def chunk(a, ci):
        return a[ci * c:(ci + 1) * c]

    q = _silu(q_raw)
    forget = lower + (1.0 - lower) * (1.0 / (1.0 + jnp.exp(-f_raw)))
    k = 1.0 - forget
    v = v_raw.astype(BF16)
    log_f = jnp.log(forget)
    hi = log_f.astype(BF16)
    rest = log_f - hi.astype(F32)
    mid = rest.astype(BF16)
    lo = (rest - mid.astype(F32)).astype(BF16)
    terms = jnp.concatenate([hi, mid, lo], axis=1)
    parts = jnp.concatenate([_dot(tril_ref[...], chunk(terms, ci)) for ci in range(n_chunks)], axis=0)
    b = parts[:, :LANES] + parts[:, LANES:2 * LANES] + parts[:, 2 * LANES:]
    b_last = [b[(ci + 1) * c - 1:(ci + 1) * c, :] for ci in range(n_chunks)]

    states = [state]
    for ci in range(n_chunks):
        k_out = (chunk(k, ci) * jnp.exp(b_last[ci] - chunk(b, ci))).astype(BF16)
        states.append(states[ci] * jnp.exp(b_last[ci]) + _dot_tn(chunk(v, ci), k_out))
    side_job()

    q_bf, k_bf = q.astype(BF16), k.astype(BF16)
    scores = [masks_ref[0] * _dot_nt(chunk(q_bf, ci), chunk(k_bf, ci)) for ci in range(n_chunks)]
    for l, h in enumerate(_hg_levels()):
        decay = jnp.exp(_hg_level_exponent(b, log_f, h)).astype(BF16)
        q_l, k_l = q_bf * decay, k_bf * decay
        for ci in range(n_chunks):
            scores[ci] = scores[ci] + masks_ref[1 + l] * _dot_nt(chunk(q_l, ci), chunk(k_l, ci))
        side_job()

    q_in = (q * jnp.exp(b)).astype(BF16)
    outs = [_dot(jnp.concatenate([scores[ci].astype(BF16), chunk(q_in, ci)], axis=1),
                 jnp.concatenate([chunk(v, ci), states[ci].astype(BF16).T], axis=0)) for ci in range(n_chunks)]
    o = _rmsnorm(jnp.concatenate(outs, axis=0), gain) * _silu(g_raw)
    for _ in side_jobs:
        _()
    return o.astype(BF16), states[n_chunks]


def _mix_in_kernel(x_ref, ngain_ref, w_ref, lbl_ref, hgain_ref, tril_ref, masks_ref, sb_ref, o_ref,
                   h_scr, hg_scr, state_ref):
    @pl.when(pl.program_id(1) == 0)
    def _():
        state_ref[...] = jnp.zeros_like(state_ref)

    heads = HG_WIDTH // HG_HEAD_DIM
    sb_cols = sb_ref.shape[1]
    h_scr[...] = _rmsnorm(x_ref[...], ngain_ref[...]).astype(BF16)

    def project_head(h, part):
        def job():
            cols = [sb_cols + (2 * part + j) * HG_WIDTH + h * HG_HEAD_DIM for j in range(2)]
            w = jnp.concatenate([w_ref[:, c0:c0 + HG_HEAD_DIM] for c0 in cols], axis=1)
            hg_scr[h, :, 2 * part * HG_HEAD_DIM:(2 * part + 2) * HG_HEAD_DIM] = _dot(h_scr[...], w)
        return job

    def project_sb(block, width=2 * LANES):
        def job():
            cols = slice(block * width, (block + 1) * width)
            p = _dot(h_scr[...], w_ref[:, cols])
            sb_ref[:, cols] = ((p * SB_Q_SCALE) if block * width < SB_WIDTH else p).astype(BF16)
        return job

    logits = lbl_ref[...]
    e = jnp.exp(logits - jnp.max(logits, axis=0, keepdims=True))
    lower = e[0:1, :] / jnp.sum(e, axis=0, keepdims=True)

    project_head(0, 0)()
    project_head(0, 1)()
    for h in range(heads):
        lanes = slice(h * HG_HEAD_DIM, (h + 1) * HG_HEAD_DIM)
        if h + 1 < heads:
            jobs = [lambda: None, project_head(h + 1, 0), lambda: None, lambda: None, lambda: None, project_head(h + 1, 1)]
        else:
            jobs = [project_sb(blk) for blk in range(sb_cols // (2 * LANES))]
        q, f, v, g = (hg_scr[h, :, j * HG_HEAD_DIM:(j + 1) * HG_HEAD_DIM] for j in range(4))
        o, state_ref[h] = _hg_head(q, f, v, g, lower[:, lanes], hgain_ref[:, lanes], state_ref[h],
                                   tril_ref, masks_ref, jobs)
        o_ref[:, lanes] = o


def _mix_in(x, gain, w_in, lb_logits, hg_gain, batch, seq):
    n, d = x.shape
    sb_cols = 3 * SB_WIDTH
    heads = HG_WIDTH // HG_HEAD_DIM
    tiles = seq // MIX_ROWS
    tril, masks = _hg_constants()
    rows = lambda width: pl.BlockSpec((MIX_ROWS, width), lambda b, t: (b * tiles + t, 0))
    return pl.pallas_call(
        _mix_in_kernel,
        grid=(batch, tiles),
        in_specs=[rows(d), _resident((1, d)), _resident(w_in.shape), _resident(lb_logits.shape),
                  _resident((1, HG_WIDTH)), _resident(tril.shape), _resident(masks.shape)],
        out_specs=[rows(sb_cols), rows(HG_WIDTH)],
        out_shape=[jax.ShapeDtypeStruct((n, sb_cols), BF16), jax.ShapeDtypeStruct((n, HG_WIDTH), BF16)],
        scratch_shapes=[pltpu.VMEM((MIX_ROWS, d), BF16),
                        pltpu.VMEM((heads, MIX_ROWS, 4 * HG_HEAD_DIM), F32),
                        pltpu.VMEM((heads, HG_HEAD_DIM, HG_HEAD_DIM), F32)],
        compiler_params=pltpu.CompilerParams(dimension_semantics=("parallel", "arbitrary"),
                                             vmem_limit_bytes=V7X_VMEM_LIMIT),
        name="mix_in",
    )(x, gain.reshape(1, d), w_in, lb_logits, hg_gain.reshape(1, HG_WIDTH),
      jnp.asarray(tril, BF16), jnp.asarray(masks, F32))


def kernel(x, ffn1_norm, ffn1_w_gate, ffn1_w_up, ffn1_w_down, mix_norm, w_in, sb_out_norm, hg_lower_bound_logits, hg_out_norm, w_out, ffn2_norm, ffn2_w_gate, ffn2_w_up, ffn2_w_down, final_norm):
    batch, seq, d = x.shape
    assert ffn1_norm.shape[0] == 1, "single-layer kernel"
    assert w_in.shape[2] == 3 * SB_WIDTH + 4 * HG_WIDTH and w_out.shape[1] == SB_WIDTH + HG_WIDTH
    assert seq % max(SB_TQ, MIX_ROWS) == 0 and (batch * seq) % FFN_ROWS == 0 and SB_TQ == 2 * SB_TK
    n = batch * seq
    bf = lambda w: w[0].astype(BF16)

    x1, w_in_bf, w_out_bf, wg2, wu2, wd2 = _ffn(
        x.reshape(n, d), ffn1_norm[0], bf(ffn1_w_gate), bf(ffn1_w_up), bf(ffn1_w_down),
        cast=(w_in[0], w_out[0], ffn2_w_gate[0], ffn2_w_up[0], ffn2_w_down[0]))
    sb, o_hg = _mix_in(x1, mix_norm[0], w_in_bf, hg_lower_bound_logits, hg_out_norm[0], batch, seq)
    o_sb = _sb_attention(sb.reshape(batch, seq, -1), sb_out_norm[0], batch, seq)
    out = _ffn(x1, ffn2_norm[0], wg2, wu2, wd2, mix=(o_sb.reshape(n, -1), o_hg, w_out_bf), final_gain=final_norm)
    return out.reshape(batch, seq, d)
```
